```python
import math
import jax, jax.numpy as jnp
from jax import lax
import numpy as np

D_MODEL = 4096
BATCH = 8
SEQ = 2048
DEPTH = 2

MEM_LEN = 256
MLA_HEADS = 16
Q_LORA = 768
KV_LORA = 512
NOPE_DIM = 128
ROPE_DIM = 64
V_DIM = 128
QK_DIM = NOPE_DIM + ROPE_DIM
MLA_WIDTH = MLA_HEADS * V_DIM
ROPE_THETA = 10000.0
Q_BLOCK = 128
RNN_WIDTH = D_MODEL // 2
RNN_BLOCKS = 16
RNN_BLOCK_DIM = RNN_WIDTH // RNN_BLOCKS
CONV_WIDTH = 4
LRU_C = 8.0
MIX_WIDTH = MLA_WIDTH + RNN_WIDTH
OFF_KV = Q_LORA
OFF_KR = OFF_KV + KV_LORA
OFF_RX = OFF_KR + ROPE_DIM
OFF_RG = OFF_RX + RNN_WIDTH
IN_WIDTH = OFF_RG + RNN_WIDTH
X_HEADS = 4
X_HEAD_DIM = 128
X_WIDTH = X_HEADS * X_HEAD_DIM
N_EXPERTS = 64
TOP_K = 8
N_GROUPS = 8
TOPK_GROUPS = 4
EXPERT_DIM = 384
SHARED_DIM = 384
ROUTED_SCALE = 2.5
MOE_BLOCK = 256
DN_ALPHA = (2.0 * DEPTH) ** 0.25
DN_BETA = (8.0 * DEPTH) ** -0.25
LN_EPS = 1e-5
RMS_EPS = 1e-6

kernel_name = 'hybrid_mla_rglru_moe_deepnorm'


def layer_norm(x, g, b):
    xf = x.astype(jnp.float32)
    mu = jnp.mean(xf, -1, keepdims=True)
    var = jnp.mean(jnp.square(xf - mu), -1, keepdims=True)
    y = (xf - mu) * lax.rsqrt(var + LN_EPS) * g.astype(jnp.float32) + b.astype(jnp.float32)
    return y.astype(x.dtype)


def rms_norm(x, g):
    xf = x.astype(jnp.float32)
    y = xf * lax.rsqrt(jnp.mean(xf * xf, -1, keepdims=True) + RMS_EPS) * g.astype(jnp.float32)
    return y.astype(x.dtype)


def rope_tables(positions):
    freqs = ROPE_THETA ** (-jnp.arange(0, ROPE_DIM, 2, dtype=jnp.float32) / ROPE_DIM)
    ang = positions.astype(jnp.float32)[..., None] * freqs
    return jnp.cos(ang), jnp.sin(ang)


def apply_rope(x, cos, sin):
    x1, x2 = jnp.split(x.astype(jnp.float32), 2, axis=-1)
    return jnp.concatenate([x1 * cos - x2 * sin, x2 * cos + x1 * sin], -1).astype(x.dtype)


def mla_group(c_q, c_kv, k_rope, cos, sin, q_norm_g, kv_norm_g, w_q_b, w_kv_b):
    B, S, _ = c_q.shape
    q = jnp.einsum('bsr,rhd->bshd', rms_norm(c_q, q_norm_g), w_q_b)
    q_nope = q[..., :NOPE_DIM]
    q_rope = apply_rope(q[..., NOPE_DIM:], cos[:, :, None, :], sin[:, :, None, :])
    kv = jnp.einsum('bsr,rhd->bshd', rms_norm(c_kv, kv_norm_g), w_kv_b)
    k_nope, v = kv[..., :NOPE_DIM], kv[..., NOPE_DIM:]
    k_rope = apply_rope(k_rope, cos, sin)
    scale = QK_DIM ** -0.5
    outs = []
    for start in range(0, S, Q_BLOCK):
        end = start + Q_BLOCK
        s = (jnp.einsum('bqhd,bkhd->bhqk', q_nope[:, start:end], k_nope[:, :end])
             + jnp.einsum('bqhr,bkr->bhqk', q_rope[:, start:end], k_rope[:, :end]))
        s = s.astype(jnp.float32) * scale
        mask = (start + jnp.arange(Q_BLOCK))[:, None] >= jnp.arange(end)[None, :]
        p = jax.nn.softmax(jnp.where(mask, s, -jnp.inf), axis=-1).astype(v.dtype)
        outs.append(jnp.einsum('bhqk,bkhd->bqhd', p, v[:, :end]))
    return jnp.concatenate(outs, axis=1).reshape(B, S, MLA_WIDTH)


def rglru_group(u, g, conv_w, conv_b, w_rg_a, b_rg_a, w_rg_x, b_rg_x, lam):
    B, S, _ = u.shape
    xc = lax.conv_general_dilated(u, conv_w, window_strides=(1,), padding=[(CONV_WIDTH - 1, 0)],
                                  dimension_numbers=('NWC', 'WIO', 'NWC'),
                                  feature_group_count=RNN_WIDTH) + conv_b
    xb = xc.reshape(B, S, RNN_BLOCKS, RNN_BLOCK_DIM)
    r = jax.nn.sigmoid(jnp.einsum('bshc,hcd->bshd', xb, w_rg_a).reshape(B, S, RNN_WIDTH) + b_rg_a)
    i = jax.nn.sigmoid(jnp.einsum('bshc,hcd->bshd', xb, w_rg_x).reshape(B, S, RNN_WIDTH) + b_rg_x)
    log_a = -LRU_C * r.astype(jnp.float32) * jax.nn.softplus(-lam.astype(jnp.float32))
    a = jnp.exp(log_a)
    b = jnp.sqrt(-jnp.expm1(2.0 * log_a)) * (i * xc).astype(jnp.float32)

    def combine(left, right):
        a_l, b_l = left
        a_r, b_r = right
        return a_r * a_l, a_r * b_l + b_r

    _, h = lax.associative_scan(combine, (a, b), axis=1)
    return (h * jax.nn.gelu(g.astype(jnp.float32))).astype(u.dtype)


def mem_cross_attn(h, mem, w_xq, w_xk, w_xv, w_xo):
    B, S, _ = h.shape
    M = mem.shape[1]
    q = (h @ w_xq).reshape(B, S, X_HEADS, X_HEAD_DIM)
    k = (mem @ w_xk).reshape(B, M, X_HEADS, X_HEAD_DIM)
    v = (mem @ w_xv).reshape(B, M, X_HEADS, X_HEAD_DIM)
    s = jnp.einsum('bqhd,bkhd->bhqk', q, k).astype(jnp.float32) * (X_HEAD_DIM ** -0.5)
    p = jax.nn.softmax(s, axis=-1).astype(v.dtype)
    o = jnp.einsum('bhqk,bkhd->bqhd', p, v).reshape(B, S, X_WIDTH)
    return o @ w_xo


def route(xf, w_router, router_bias):
    T = xf.shape[0]
    scores = jax.nn.sigmoid(xf.astype(jnp.float32) @ w_router.astype(jnp.float32))
    biased = scores + router_bias.astype(jnp.float32)
    per_group = N_EXPERTS // N_GROUPS
    group_score = lax.top_k(biased.reshape(T, N_GROUPS, per_group), 2)[0].sum(-1)
    _, top_groups = lax.top_k(group_score, TOPK_GROUPS)
    group_mask = jnp.any(top_groups[..., None] == jnp.arange(N_GROUPS), axis=-2)
    masked = jnp.where(jnp.repeat(group_mask, per_group, axis=-1), biased, -jnp.inf)
    _, idx = lax.top_k(masked, TOP_K)
    gate = jnp.take_along_axis(scores, idx, axis=-1)
    gate = gate / jnp.sum(gate, -1, keepdims=True) * ROUTED_SCALE
    return idx, gate


def routed_experts(xf, idx, gate, w_e1, w_e3, w_e2):
    T = xf.shape[0]
    M = T * TOP_K
    e_flat = idx.reshape(M)
    tok_flat = jnp.arange(M, dtype=jnp.int32) // TOP_K
    g_flat = gate.reshape(M)
    order = jnp.argsort(e_flat)
    e_sorted = e_flat[order]
    counts = jnp.bincount(e_flat, length=N_EXPERTS)
    padded = (counts + MOE_BLOCK - 1) // MOE_BLOCK * MOE_BLOCK
    start = jnp.cumsum(counts) - counts
    pend = jnp.cumsum(padded)
    pstart = pend - padded
    dest = pstart[e_sorted] + (jnp.arange(M, dtype=jnp.int32) - start[e_sorted])
    n_blocks = -(-M // MOE_BLOCK) + N_EXPERTS
    P = n_blocks * MOE_BLOCK
    row_tok = jnp.zeros((P,), jnp.int32).at[dest].set(tok_flat[order])
    row_gate = jnp.zeros((P,), jnp.float32).at[dest].set(g_flat[order])
    block_e = jnp.minimum(jnp.searchsorted(pend, jnp.arange(n_blocks) * MOE_BLOCK, side='right'),
                          N_EXPERTS - 1)

    def body(acc, blk):
        tok, gt, e = blk
        xb = xf[tok]
        hb = jax.nn.silu(xb @ w_e1[e]) * (xb @ w_e3[e])
        yb = (hb @ w_e2[e]) * gt[:, None].astype(xb.dtype)
        return acc.at[tok].add(yb), None

    acc, _ = lax.scan(body, jnp.zeros_like(xf),
                      (row_tok.reshape(n_blocks, MOE_BLOCK), row_gate.reshape(n_blocks, MOE_BLOCK), block_e))
    return acc


def moe(h, w_router, router_bias, w_e1, w_e3, w_e2, w_s1, w_s3, w_s2):
    B, S, D = h.shape
    xf = h.reshape(B * S, D)
    idx, gate = route(xf, w_router, router_bias)
    shared = (jax.nn.silu(xf @ w_s1) * (xf @ w_s3)) @ w_s2
    return (routed_experts(xf, idx, gate, w_e1, w_e3, w_e2) + shared).reshape(B, S, D)


def setup_inputs(seed: int = 0) -> dict:
    key = jax.random.key(seed)
    ks = iter(jax.random.split(key, 48))
    L = DEPTH

    def nrm(shape, scale):
        return jax.random.normal(next(ks), shape, jnp.float32) * scale

    def gain(shape):
        return 1.0 + 0.02 * jax.random.normal(next(ks), shape, jnp.float32)

    x = nrm((BATCH, SEQ, D_MODEL), 1.0)
    mem = nrm((BATCH, MEM_LEN, D_MODEL), 1.0)
    positions = (jax.random.randint(next(ks), (BATCH, 1), 0, 1024, jnp.int32)
                 + jnp.arange(SEQ, dtype=jnp.int32)[None, :])
    w_kv_b = nrm((L, KV_LORA, MLA_HEADS, NOPE_DIM + V_DIM), KV_LORA ** -0.5)
    w_kv_b = w_kv_b.at[..., NOPE_DIM:].multiply(DN_BETA)
    a_c = jax.random.uniform(next(ks), (L, RNN_WIDTH), jnp.float32, 0.9, 0.999)
    s = a_c ** (1.0 / LRU_C)
    lru_lambda = jnp.log(s) - jnp.log1p(-s)
    return {
        'x': x, 'mem': mem, 'positions': positions,
        'ln_in_g': gain((D_MODEL,)), 'ln_in_b': nrm((D_MODEL,), 0.02),
        'w_in': nrm((L, D_MODEL, IN_WIDTH), D_MODEL ** -0.5),
        'q_norm_g': gain((L, Q_LORA)), 'kv_norm_g': gain((L, KV_LORA)),
        'w_q_b': nrm((L, Q_LORA, MLA_HEADS, QK_DIM), Q_LORA ** -0.5),
        'w_kv_b': w_kv_b,
        'conv_w': nrm((L, CONV_WIDTH, 1, RNN_WIDTH), CONV_WIDTH ** -0.5),
        'conv_b': nrm((L, RNN_WIDTH), 0.02),
        'w_rg_a': nrm((L, RNN_BLOCKS, RNN_BLOCK_DIM, RNN_BLOCK_DIM), RNN_BLOCK_DIM ** -0.5),
        'b_rg_a': nrm((L, RNN_WIDTH), 0.02),
        'w_rg_x': nrm((L, RNN_BLOCKS, RNN_BLOCK_DIM, RNN_BLOCK_DIM), RNN_BLOCK_DIM ** -0.5),
        'b_rg_x': nrm((L, RNN_WIDTH), 0.02),
        'lru_lambda': lru_lambda,
        'mla_out_g': gain((L, MLA_WIDTH)), 'rnn_out_g': gain((L, RNN_WIDTH)),
        'w_o': nrm((L, MIX_WIDTH, D_MODEL), MIX_WIDTH ** -0.5 * DN_BETA),
        'ln1_g': gain((L, D_MODEL)), 'ln1_b': nrm((L, D_MODEL), 0.02),
        'w_xq': nrm((L, D_MODEL, X_WIDTH), D_MODEL ** -0.5),
        'w_xk': nrm((L, D_MODEL, X_WIDTH), D_MODEL ** -0.5),
        'w_xv': nrm((L, D_MODEL, X_WIDTH), D_MODEL ** -0.5 * DN_BETA),
        'w_xo': nrm((L, X_WIDTH, D_MODEL), X_WIDTH ** -0.5 * DN_BETA),
        'ln2_g': gain((L, D_MODEL)), 'ln2_b': nrm((L, D_MODEL), 0.02),
        'w_router': nrm((L, D_MODEL, N_EXPERTS), D_MODEL ** -0.5),
        'router_bias': nrm((L, N_EXPERTS), 0.01),
        'w_e1': nrm((L, N_EXPERTS, D_MODEL, EXPERT_DIM), D_MODEL ** -0.5),
        'w_e3': nrm((L, N_EXPERTS, D_MODEL, EXPERT_DIM), D_MODEL ** -0.5),
        'w_e2': nrm((L, N_EXPERTS, EXPERT_DIM, D_MODEL), EXPERT_DIM ** -0.5 * DN_BETA),
        'w_s1': nrm((L, D_MODEL, SHARED_DIM), D_MODEL ** -0.5),
        'w_s3': nrm((L, D_MODEL, SHARED_DIM), D_MODEL ** -0.5),
        'w_s2': nrm((L, SHARED_DIM, D_MODEL), SHARED_DIM ** -0.5 * DN_BETA),
        'ln3_g': gain((L, D_MODEL)), 'ln3_b': nrm((L, D_MODEL), 0.02),
    }


def reference(x, mem, positions, ln_in_g, ln_in_b, w_in, q_norm_g, kv_norm_g, w_q_b, w_kv_b,
              conv_w, conv_b, w_rg_a, b_rg_a, w_rg_x, b_rg_x, lru_lambda, mla_out_g, rnn_out_g,
              w_o, ln1_g, ln1_b, w_xq, w_xk, w_xv, w_xo, ln2_g, ln2_b, w_router, router_bias,
              w_e1, w_e3, w_e2, w_s1, w_s3, w_s2, ln3_g, ln3_b):
    cos, sin = rope_tables(positions)
    h = layer_norm(x, ln_in_g, ln_in_b)
    for l in range(DEPTH):
        proj = h @ w_in[l]
        attn = mla_group(proj[..., :OFF_KV], proj[..., OFF_KV:OFF_KR], proj[..., OFF_KR:OFF_RX],
                         cos, sin, q_norm_g[l], kv_norm_g[l], w_q_b[l], w_kv_b[l])
        rec = rglru_group(proj[..., OFF_RX:OFF_RG], proj[..., OFF_RG:], conv_w[l], conv_b[l],
                          w_rg_a[l], b_rg_a[l], w_rg_x[l], b_rg_x[l], lru_lambda[l])
        mixed = jnp.concatenate([rms_norm(attn, mla_out_g[l]), rms_norm(rec, rnn_out_g[l])], -1) @ w_o[l]
        h = layer_norm(DN_ALPHA * h + mixed, ln1_g[l], ln1_b[l])
        h = layer_norm(DN_ALPHA * h + mem_cross_attn(h, mem, w_xq[l], w_xk[l], w_xv[l], w_xo[l]),
                       ln2_g[l], ln2_b[l])
        h = layer_norm(DN_ALPHA * h + moe(h, w_router[l], router_bias[l], w_e1[l], w_e3[l], w_e2[l],
                                          w_s1[l], w_s3[l], w_s2[l]),
                       ln3_g[l], ln3_b[l])
    return h
```

```python
import functools
import math

import jax
import jax.numpy as jnp
from jax import lax
from jax.experimental import pallas as pl
from jax.experimental.pallas import tpu as pltpu

F32 = jnp.float32
BF16 = jnp.bfloat16
U32 = jnp.uint32

LANE = 128
VMEM_LIMIT = 56 * 1024 * 1024

ROPE_THETA = 10000.0
X_HEAD_DIM = 128
TOP_K = 8
N_GROUPS = 8
TOPK_GROUPS = 4
ROUTED_SCALE = 2.5
LRU_C = 8.0
LN_EPS = 1e-5
RMS_EPS = 1e-6
MOE_ROWS = 256


def _params(*sem):
    return pltpu.CompilerParams(dimension_semantics=sem, vmem_limit_bytes=VMEM_LIMIT)


def _layer_norm(y, g, b):
    mu = jnp.mean(y, -1, keepdims=True)
    yc = y - mu
    var = jnp.mean(yc * yc, -1, keepdims=True)
    return yc * lax.rsqrt(var + LN_EPS) * g + b


def _sigmoid(x):
    return 1.0 / (1.0 + jnp.exp(-x))


def _bits(x):
    return lax.bitcast_convert_type(x, U32)


def _pack_pair(lo, hi):
    lo_b = _bits(lo.astype(BF16).astype(F32))
    hi_b = _bits(hi.astype(BF16).astype(F32))
    return hi_b | lax.shift_right_logical(lo_b, jnp.uint32(16))


def _unpack_pair(w):
    lo = lax.bitcast_convert_type(lax.shift_left(w, jnp.uint32(16)), F32)
    hi = lax.bitcast_convert_type(w & jnp.uint32(0xFFFF0000), F32)
    return lo, hi


def _ln_in_kernel(x_ref, g_ref, b_ref, hf_ref, hb_ref):
    y = _layer_norm(x_ref[...], g_ref[...], b_ref[...])
    hf_ref[...] = y
    hb_ref[...] = y.astype(BF16)


def _ln_in(x, g, b, tm=256):
    T, D = x.shape
    row = pl.BlockSpec((tm, D), lambda i: (i, 0))
    vec = pl.BlockSpec((1, D), lambda i: (0, 0))
    return pl.pallas_call(
        _ln_in_kernel, grid=(T // tm,),
        in_specs=[row, vec, vec], out_specs=[row, row],
        out_shape=[jax.ShapeDtypeStruct((T, D), F32), jax.ShapeDtypeStruct((T, D), BF16)],
        compiler_params=_params("arbitrary"), name="ln_in",
    )(x, g.reshape(1, D), b.reshape(1, D))


def _mm_kernel(x_ref, w_ref, o_ref):
    o_ref[...] = jnp.dot(x_ref[...].astype(BF16), w_ref[...],
                         preferred_element_type=F32).astype(o_ref.dtype)


def _matmul(x, w, tm, tn, name):
    M, K = x.shape
    N = w.shape[1]
    return pl.pallas_call(
        _mm_kernel, grid=(M // tm, N // tn),
        in_specs=[pl.BlockSpec((tm, K), lambda i, j: (i, 0)), pl.BlockSpec((K, tn), lambda i, j: (0, j))],
        out_specs=pl.BlockSpec((tm, tn), lambda i, j: (i, j)),
        out_shape=jax.ShapeDtypeStruct((M, N), BF16),
        compiler_params=_params("arbitrary", "arbitrary"), name=name,
    )(x, w)


def _rope_kernel(pos_ref, freq_ref, cs_ref):
    ang = pos_ref[...].astype(F32) * freq_ref[...]
    keep = lax.broadcasted_iota(jnp.int32, ang.shape, 1) < LANE // 2
    cs_ref[:, :LANE] = jnp.where(keep, jnp.cos(ang), 0.0)
    cs_ref[:, LANE:] = jnp.where(keep, jnp.sin(ang), 0.0)


def _rope_tables(positions, rope_dim, tm=2048):
    T = positions.size
    half = rope_dim // 2
    freqs = ROPE_THETA ** (-jnp.arange(0, rope_dim, 2, dtype=F32) / rope_dim)
    freq_row = jnp.concatenate([freqs, freqs, jnp.zeros((LANE - 2 * half,), F32)]).reshape(1, LANE)
    tm = min(tm, T)
    return pl.pallas_call(
        _rope_kernel, grid=(T // tm,),
        in_specs=[pl.BlockSpec((tm, 1), lambda i: (i, 0)), pl.BlockSpec((1, LANE), lambda i: (0, 0))],
        out_specs=pl.BlockSpec((tm, 2 * LANE), lambda i: (i, 0)),
        out_shape=jax.ShapeDtypeStruct((T, 2 * LANE), F32),
        compiler_params=_params("arbitrary"), name="rope_tables",
    )(positions.reshape(T, 1), freq_row)


def _rope_lanes(v, cm, sm):
    return v * cm + pltpu.roll(v, LANE // 2, axis=1) * sm


def _q_up_kernel(x_ref, g_ref, cs_ref, w_ref, o_ref, xn_ref, *, scale):
    @pl.when(pl.program_id(1) == 0)
    def _():
        x = x_ref[...].astype(F32)
        r = lax.rsqrt(jnp.mean(x * x, -1, keepdims=True) + RMS_EPS)
        xn_ref[...] = (x * r * g_ref[...]).astype(BF16)

    y = jnp.dot(xn_ref[...], w_ref[...], preferred_element_type=F32)
    cm = cs_ref[:, :LANE] * scale
    sm = cs_ref[:, LANE:] * scale
    for h in range(y.shape[1] // (2 * LANE)):
        c0 = h * 2 * LANE
        o_ref[:, c0:c0 + LANE] = (y[:, c0:c0 + LANE] * scale).astype(BF16)
        o_ref[:, c0 + LANE:c0 + 2 * LANE] = _rope_lanes(y[:, c0 + LANE:c0 + 2 * LANE], cm, sm).astype(BF16)


def _q_up(proj, col_block, q_norm_g, cs, wq, scale, tm=512, tn=1024):
    T = proj.shape[0]
    R, N = wq.shape
    tn = min(tn, N)
    return pl.pallas_call(
        functools.partial(_q_up_kernel, scale=scale), grid=(T // tm, N // tn),
        in_specs=[pl.BlockSpec((tm, R), lambda i, j: (i, col_block)),
                  pl.BlockSpec((1, R), lambda i, j: (0, 0)),
                  pl.BlockSpec((tm, 2 * LANE), lambda i, j: (i, 0)),
                  pl.BlockSpec((R, tn), lambda i, j: (0, j))],
        out_specs=pl.BlockSpec((tm, tn), lambda i, j: (i, j)),
        out_shape=jax.ShapeDtypeStruct((T, N), BF16),
        scratch_shapes=[pltpu.VMEM((tm, R), BF16)],
        compiler_params=_params("arbitrary", "arbitrary"), name="q_up",
    )(proj, q_norm_g.reshape(1, R), cs, wq)


def _kv_up_kernel(x_ref, g_ref, kr_ref, cs_ref, w_ref, o_ref, kro_ref, xn_ref):
    @pl.when(pl.program_id(1) == 0)
    def _():
        x = x_ref[...].astype(F32)
        r = lax.rsqrt(jnp.mean(x * x, -1, keepdims=True) + RMS_EPS)
        xn_ref[...] = (x * r * g_ref[...]).astype(BF16)
        kro_ref[...] = _rope_lanes(kr_ref[...].astype(F32), cs_ref[:, :LANE], cs_ref[:, LANE:]).astype(BF16)

    o_ref[...] = jnp.dot(xn_ref[...], w_ref[...], preferred_element_type=F32).astype(BF16)


def _kv_up(proj, kv_block, kr_block, kv_norm_g, cs, wkv, tm=512, tn=1024):
    T = proj.shape[0]
    R, N = wkv.shape
    tn = min(tn, N)
    return pl.pallas_call(
        _kv_up_kernel, grid=(T // tm, N // tn),
        in_specs=[pl.BlockSpec((tm, R), lambda i, j: (i, kv_block)),
                  pl.BlockSpec((1, R), lambda i, j: (0, 0)),
                  pl.BlockSpec((tm, LANE), lambda i, j: (i, kr_block)),
                  pl.BlockSpec((tm, 2 * LANE), lambda i, j: (i, 0)),
                  pl.BlockSpec((R, tn), lambda i, j: (0, j))],
        out_specs=[pl.BlockSpec((tm, tn), lambda i, j: (i, j)),
                   pl.BlockSpec((tm, LANE), lambda i, j: (i, 0))],
        out_shape=[jax.ShapeDtypeStruct((T, N), BF16), jax.ShapeDtypeStruct((T, LANE), BF16)],
        scratch_shapes=[pltpu.VMEM((tm, R), BF16)],
        compiler_params=_params("arbitrary", "arbitrary"), name="kv_up",
    )(proj, kv_norm_g.reshape(1, R), proj, cs, wkv)


def _attn_kernel(q_ref, kn_ref, kr_ref, v_ref, o_ref, k_ref, *, tq):
    S = q_ref.shape[0]
    k_ref[:, :LANE] = kn_ref[...]
    k_ref[:, LANE:] = kr_ref[...]
    nt = (((1,), (1,)), ((), ()))
    row = lax.broadcasted_iota(jnp.int32, (tq, tq), 0)
    col = lax.broadcasted_iota(jnp.int32, (tq, tq), 1)
    tri = row >= col
    for qi in range(S // tq):
        lo = qi * tq
        q = q_ref[lo:lo + tq, :]
        s_d = lax.dot_general(q, k_ref[lo:lo + tq, :], nt, preferred_element_type=F32)
        s_d = jnp.where(tri, s_d, -jnp.inf)
        m = jnp.max(s_d, -1, keepdims=True)
        if qi > 0:
            s_o = lax.dot_general(q, k_ref[:lo, :], nt, preferred_element_type=F32)
            m = jnp.maximum(m, jnp.max(s_o, -1, keepdims=True))
        p_d = jnp.exp(s_d - m)
        l = jnp.sum(p_d, -1, keepdims=True)
        acc = jnp.dot(p_d.astype(BF16), v_ref[lo:lo + tq, :], preferred_element_type=F32)
        if qi > 0:
            p_o = jnp.exp(s_o - m)
            l = l + jnp.sum(p_o, -1, keepdims=True)
            acc = acc + jnp.dot(p_o.astype(BF16), v_ref[:lo, :], preferred_element_type=F32)
        o_ref[lo:lo + tq, :] = (acc / l).astype(BF16)


def _attention(q, kv, kr, B, S, H, tq=256):
    T = B * S
    tq = min(tq, S)
    return pl.pallas_call(
        functools.partial(_attn_kernel, tq=tq), grid=(B, H),
        in_specs=[pl.BlockSpec((S, 2 * LANE), lambda b, h: (b, h)),
                  pl.BlockSpec((S, LANE), lambda b, h: (b, h)),
                  pl.BlockSpec((S, LANE), lambda b, h: (b, 0)),
                  pl.BlockSpec((S, LANE), lambda b, h: (b, H + h))],
        out_specs=pl.BlockSpec((S, LANE), lambda b, h: (b, h)),
        out_shape=jax.ShapeDtypeStruct((T, H * LANE), BF16),
        scratch_shapes=[pltpu.VMEM((S, 2 * LANE), BF16)],
        compiler_params=_params("arbitrary", "arbitrary"), name="mla_attention",
    )(q, kv, kr, kv)


def _rglru_kernel(u_ref, g_ref, cw_ref, cb_ref, wa_ref, ba_ref, wx_ref, bx_ref, lam_ref, o_ref,
                  a_ref, b_ref):
    S, tc = u_ref.shape
    u = u_ref[...].astype(F32)
    row = lax.broadcasted_iota(jnp.int32, (S, tc), 0)
    cw = cw_ref[...]
    nconv = cw.shape[0]
    xc = u * cw[nconv - 1:nconv, :] + cb_ref[...]
    for d in range(1, nconv):
        xc = xc + jnp.where(row >= d, pltpu.roll(u, d, axis=0), 0.0) * cw[nconv - 1 - d:nconv - d, :]
    lam = lam_ref[...]
    z = -lam
    softplus = jnp.maximum(z, 0.0) + jnp.log1p(jnp.exp(-jnp.abs(z)))
    for blk in range(tc // LANE):
        sl = slice(blk * LANE, (blk + 1) * LANE)
        xb = xc[:, sl]
        xbb = xb.astype(BF16)
        r = _sigmoid(jnp.dot(xbb, wa_ref[blk], preferred_element_type=F32) + ba_ref[:, sl])
        gi = _sigmoid(jnp.dot(xbb, wx_ref[blk], preferred_element_type=F32) + bx_ref[:, sl])
        log_a = (-LRU_C) * r * softplus[:, sl]
        a_blk = jnp.exp(log_a)
        a_ref[:, sl] = a_blk
        b_ref[:, sl] = jnp.sqrt(-jnp.tanh(log_a) * (a_blk * a_blk + 1.0)) * (gi * xb)

    a = a_ref[...]
    b = b_ref[...]
    sub = row & 7
    for d in (1, 2, 4):
        ok = sub >= d
        b = b + a * jnp.where(ok, pltpu.roll(b, d, axis=0), 0.0)
        a = a * jnp.where(ok, pltpu.roll(a, d, axis=0), 1.0)
    a_ref[...] = a
    b_ref[...] = b

    def carry(t, hprev):
        r0 = pl.multiple_of(t * 8, 8)
        h = b_ref[pl.ds(r0, 8), :] + a_ref[pl.ds(r0, 8), :] * hprev
        b_ref[pl.ds(r0, 8), :] = h
        return jnp.broadcast_to(h[7:8, :], h.shape)

    lax.fori_loop(0, S // 8, carry, jnp.zeros((8, tc), F32))
    gg = g_ref[...].astype(F32)
    gelu = 0.5 * gg * (1.0 + jnp.tanh(math.sqrt(2.0 / math.pi) * (gg + 0.044715 * (gg * gg * gg))))
    o_ref[...] = (b_ref[...] * gelu).astype(BF16)


def _rglru(proj, B, S, C, conv_w, conv_b, wa, ba, wx, bx, lam, tc=256):
    T = B * S
    nb = tc // LANE
    ncb = C // tc
    vec = pl.BlockSpec((1, tc), lambda b, j: (0, j))
    wspec = pl.BlockSpec((nb, LANE, LANE), lambda b, j: (j, 0, 0))
    nconv = conv_w.shape[0]
    return pl.pallas_call(
        _rglru_kernel, grid=(B, ncb),
        in_specs=[pl.BlockSpec((S, tc), lambda b, j: (b, j)),
                  pl.BlockSpec((S, tc), lambda b, j: (b, ncb + j)),
                  pl.BlockSpec((nconv, tc), lambda b, j: (0, j)),
                  vec, wspec, vec, wspec, vec, vec],
        out_specs=pl.BlockSpec((S, tc), lambda b, j: (b, j)),
        out_shape=jax.ShapeDtypeStruct((T, C), BF16),
        scratch_shapes=[pltpu.VMEM((S, tc), F32), pltpu.VMEM((S, tc), F32)],
        compiler_params=_params("arbitrary", "arbitrary"), name="rglru",
    )(proj, proj, conv_w, conv_b.reshape(1, C), wa, ba.reshape(1, C), wx, bx.reshape(1, C),
      lam.reshape(1, C))


def _out_proj_kernel(a_ref, r_ref, ga_ref, gr_ref, hf_ref, w_ref, lg_ref, lb_ref, of_ref, ob_ref,
                     xs_ref, acc_ref, *, alpha):
    j = pl.program_id(1)
    nj = pl.num_programs(1)
    W = a_ref.shape[1]

    @pl.when(j == 0)
    def _():
        for src, g, c0 in ((a_ref, ga_ref, 0), (r_ref, gr_ref, W)):
            x = src[...].astype(F32)
            r = lax.rsqrt(jnp.mean(x * x, -1, keepdims=True) + RMS_EPS)
            xs_ref[:, c0:c0 + W] = (x * r * g[...]).astype(BF16)

    acc_ref[j] = jnp.dot(xs_ref[...], w_ref[...], preferred_element_type=F32)

    @pl.when(j == nj - 1)
    def _():
        tn = acc_ref.shape[2]
        for c in range(acc_ref.shape[0]):
            of_ref[:, c * tn:(c + 1) * tn] = alpha * hf_ref[:, c * tn:(c + 1) * tn] + acc_ref[c]
        y = _layer_norm(of_ref[...], lg_ref[...], lb_ref[...])
        of_ref[...] = y
        ob_ref[...] = y.astype(BF16)


def _out_proj(attn, rec, ga, gr, hf, w_o, lg, lb, alpha, tm=256, tn=512):
    T, W = attn.shape
    D = w_o.shape[1]
    tn = min(tn, D)
    row = pl.BlockSpec((tm, D), lambda i, j: (i, 0))
    half = pl.BlockSpec((tm, W), lambda i, j: (i, 0))
    vecw = pl.BlockSpec((1, W), lambda i, j: (0, 0))
    vecd = pl.BlockSpec((1, D), lambda i, j: (0, 0))
    return pl.pallas_call(
        functools.partial(_out_proj_kernel, alpha=alpha), grid=(T // tm, D // tn),
        in_specs=[half, half, vecw, vecw, row, pl.BlockSpec((2 * W, tn), lambda i, j: (0, j)), vecd, vecd],
        out_specs=[row, row],
        out_shape=[jax.ShapeDtypeStruct((T, D), F32), jax.ShapeDtypeStruct((T, D), BF16)],
        scratch_shapes=[pltpu.VMEM((tm, 2 * W), BF16), pltpu.VMEM((D // tn, tm, tn), F32)],
        compiler_params=_params("arbitrary", "arbitrary"), name="out_proj_ln",
    )(attn, rec, ga.reshape(1, W), gr.reshape(1, W), hf, w_o, lg.reshape(1, D), lb.reshape(1, D))


def _xattn_kernel(hb_ref, hf_ref, kv_ref, wq_ref, wo_ref, lg_ref, lb_ref, of_ref, op_ref, *, alpha, scale):
    XW = wq_ref.shape[1]
    D = hf_ref.shape[1]
    q = (jnp.dot(hb_ref[...], wq_ref[...], preferred_element_type=F32) * scale).astype(BF16)
    nt = (((1,), (1,)), ((), ()))
    outs = []
    for h in range(XW // X_HEAD_DIM):
        sl = slice(h * X_HEAD_DIM, (h + 1) * X_HEAD_DIM)
        s = lax.dot_general(q[:, sl], kv_ref[:, sl], nt, preferred_element_type=F32)
        m = jnp.max(s, -1, keepdims=True)
        p = jnp.exp(s - m)
        l = jnp.sum(p, -1, keepdims=True)
        o = jnp.dot(p.astype(BF16), kv_ref[:, XW + h * X_HEAD_DIM:XW + (h + 1) * X_HEAD_DIM],
                    preferred_element_type=F32)
        outs.append((o / l).astype(BF16))
    o = jnp.concatenate(outs, axis=-1)
    y = alpha * hf_ref[...] + jnp.dot(o, wo_ref[...], preferred_element_type=F32)
    y = _layer_norm(y, lg_ref[...], lb_ref[...])
    of_ref[...] = y
    op_ref[...] = _pack_pair(y[:, :D // 2], y[:, D // 2:])


def _xattn(hb, hf, kvm, wq, wo, lg, lb, alpha, S, M, tm=256):
    T, D = hf.shape
    XW = wq.shape[1]
    per_b = S // tm
    row = pl.BlockSpec((tm, D), lambda i: (i, 0))
    vecd = pl.BlockSpec((1, D), lambda i: (0, 0))
    return pl.pallas_call(
        functools.partial(_xattn_kernel, alpha=alpha, scale=X_HEAD_DIM ** -0.5), grid=(T // tm,),
        in_specs=[row, row, pl.BlockSpec((M, 2 * XW), lambda i: (i // per_b, 0)),
                  pl.BlockSpec((D, XW), lambda i: (0, 0)), pl.BlockSpec((XW, D), lambda i: (0, 0)),
                  vecd, vecd],
        out_specs=[row, pl.BlockSpec((tm, D // 2), lambda i: (i, 0))],
        out_shape=[jax.ShapeDtypeStruct((T, D), F32), jax.ShapeDtypeStruct((T, D // 2), U32)],
        compiler_params=_params("arbitrary"), name="mem_xattn_ln",
    )(hb, hf, kvm, wq, wo, lg.reshape(1, D), lb.reshape(1, D))


def _router_kernel(hf_ref, whi_ref, wlo_ref, bias_ref, idx_ref, gate_ref):
    x = hf_ref[...]
    xh = x.astype(BF16)
    xl = (x - xh.astype(F32)).astype(BF16)
    whi = whi_ref[...]
    logits = (jnp.dot(xh, whi, preferred_element_type=F32)
              + (jnp.dot(xl, whi, preferred_element_type=F32)
                 + jnp.dot(xh, wlo_ref[...], preferred_element_type=F32)))
    E = bias_ref.shape[0]
    tm = x.shape[0]
    per = E // N_GROUPS
    scores = _sigmoid(logits.T[:E, :])
    biased = scores + bias_ref[...]
    neg = -jnp.inf
    b3 = biased.reshape(N_GROUPS, per, tm)
    i3 = lax.broadcasted_iota(jnp.int32, b3.shape, 1)
    m1 = jnp.max(b3, axis=1, keepdims=True)
    first = jnp.min(jnp.where(b3 == m1, i3, per), axis=1, keepdims=True)
    m2 = jnp.max(jnp.where(i3 == first, neg, b3), axis=1, keepdims=True)
    gscore = (m1 + m2).reshape(N_GROUPS, tm)
    gi = lax.broadcasted_iota(jnp.int32, gscore.shape, 0)
    gsel = jnp.zeros(gscore.shape, F32)
    work = gscore
    for _ in range(TOPK_GROUPS):
        m = jnp.max(work, axis=0, keepdims=True)
        pick = jnp.min(jnp.where(work == m, gi, N_GROUPS), axis=0, keepdims=True)
        hit = gi == pick
        gsel = jnp.where(hit, 1.0, gsel)
        work = jnp.where(hit, neg, work)
    keep = jnp.broadcast_to(gsel.reshape(N_GROUPS, 1, tm), b3.shape).reshape(E, tm)
    masked = jnp.where(keep > 0.0, biased, neg)
    ei = lax.broadcasted_iota(jnp.int32, masked.shape, 0)
    ids, gates = [], []
    for _ in range(TOP_K):
        m = jnp.max(masked, axis=0, keepdims=True)
        pick = jnp.min(jnp.where(masked == m, ei, E), axis=0, keepdims=True)
        hit = ei == pick
        ids.append(pick)
        gates.append(jnp.sum(jnp.where(hit, scores, 0.0), axis=0, keepdims=True))
        masked = jnp.where(hit, neg, masked)
    g = jnp.concatenate(gates, axis=0)
    idx_ref[...] = jnp.concatenate(ids, axis=0)
    gate_ref[...] = g / jnp.sum(g, axis=0, keepdims=True) * ROUTED_SCALE


def _router(hf, w_router, bias, tm=256):
    T, D = hf.shape
    E = w_router.shape[1]
    wpad = jnp.pad(w_router, ((0, 0), (0, LANE - E)))
    whi = wpad.astype(BF16)
    wlo = (wpad - whi.astype(F32)).astype(BF16)
    wspec = pl.BlockSpec((D, LANE), lambda i: (0, 0))
    ospec = pl.BlockSpec((TOP_K, tm), lambda i: (0, i))
    return pl.pallas_call(
        _router_kernel, grid=(T // tm,),
        in_specs=[pl.BlockSpec((tm, D), lambda i: (i, 0)), wspec, wspec, pl.BlockSpec((E, 1), lambda i: (0, 0))],
        out_specs=[ospec, ospec],
        out_shape=[jax.ShapeDtypeStruct((TOP_K, T), jnp.int32), jax.ShapeDtypeStruct((TOP_K, T), F32)],
        compiler_params=_params("arbitrary"), name="router",
    )(hf, whi, wlo, bias.reshape(E, 1))


def _moe_kernel(be_ref, pos0_ref, nv_ref, sm_ref, x_hbm, w13_ref, w2_ref, y_hbm, xbuf, ybuf, gsem, ssem,
                *, n_tok):
    i = pl.program_id(0)
    nb = pl.num_programs(0)
    slot = lax.rem(i, 2)
    R, D2 = xbuf.shape[1], xbuf.shape[2]
    F = w2_ref.shape[0]
    shift = TOP_K.bit_length() - 1

    def slot_entry(blk, r):
        nv = nv_ref[blk]
        rr = jnp.maximum(jnp.minimum(r, nv - 1), 0)
        return sm_ref[pos0_ref[blk] + rr], nv

    def gather_start(blk, sl):
        for r in range(R):
            m, _ = slot_entry(blk, r)
            tok = lax.shift_right_logical(m, shift)
            pltpu.make_async_copy(x_hbm.at[pl.ds(tok, 1), :], xbuf.at[sl, pl.ds(r, 1), :], gsem.at[sl]).start()

    def gather_wait(sl):
        for r in range(R):
            pltpu.make_async_copy(x_hbm.at[pl.ds(0, 1), :], xbuf.at[sl, pl.ds(r, 1), :], gsem.at[sl]).wait()

    def scatter_start(blk, sl):
        for r in range(R):
            m, nv = slot_entry(blk, r)
            dest = (m & (TOP_K - 1)) * n_tok + lax.shift_right_logical(m, shift)
            dest = jnp.where(r < nv, dest, TOP_K * n_tok + sl * R + r)
            pltpu.make_async_copy(ybuf.at[sl, pl.ds(r, 1), :], y_hbm.at[pl.ds(dest, 1), :], ssem.at[sl]).start()

    def scatter_wait(sl):
        for r in range(R):
            pltpu.make_async_copy(ybuf.at[sl, pl.ds(r, 1), :], y_hbm.at[pl.ds(0, 1), :], ssem.at[sl]).wait()

    @pl.when(i == 0)
    def _():
        gather_start(0, 0)

    @pl.when(i + 1 < nb)
    def _():
        gather_start(i + 1, 1 - slot)

    gather_wait(slot)
    xlo, xhi = _unpack_pair(xbuf[slot])
    h13 = (jnp.dot(xlo.astype(BF16), w13_ref[:D2, :], preferred_element_type=F32)
           + jnp.dot(xhi.astype(BF16), w13_ref[D2:, :], preferred_element_type=F32))
    h1 = h13[:, :F]
    hid = (h1 * _sigmoid(h1) * h13[:, F:]).astype(BF16)
    y = jnp.dot(hid, w2_ref[...], preferred_element_type=F32)

    @pl.when(i >= 2)
    def _():
        scatter_wait(slot)

    ybuf[slot] = _pack_pair(y[:, :D2], y[:, D2:])
    scatter_start(i, slot)

    @pl.when(i == nb - 1)
    def _():
        @pl.when(nb >= 2)
        def _():
            scatter_wait(1 - slot)
        scatter_wait(slot)


def _moe(hp, block_e, block_pos0, block_nv, sorted_m, w13, w2, n_blocks):
    T, D2 = hp.shape
    E, D, F2 = w13.shape
    F = F2 // 2
    grid_spec = pltpu.PrefetchScalarGridSpec(
        num_scalar_prefetch=4, grid=(n_blocks,),
        in_specs=[pl.BlockSpec(memory_space=pl.ANY),
                  pl.BlockSpec((None, D, F2), lambda i, be, p0, nv, sm: (be[i], 0, 0)),
                  pl.BlockSpec((None, F, D), lambda i, be, p0, nv, sm: (be[i], 0, 0))],
        out_specs=pl.BlockSpec(memory_space=pl.ANY),
        scratch_shapes=[pltpu.VMEM((2, MOE_ROWS, D2), U32), pltpu.VMEM((2, MOE_ROWS, D2), U32),
                        pltpu.SemaphoreType.DMA((2,)), pltpu.SemaphoreType.DMA((2,))])
    return pl.pallas_call(
        functools.partial(_moe_kernel, n_tok=T), grid_spec=grid_spec,
        out_shape=jax.ShapeDtypeStruct((TOP_K * T + 2 * MOE_ROWS, D2), U32),
        compiler_params=_params("arbitrary"), name="moe_experts",
    )(block_e, block_pos0, block_nv, sorted_m, hp, w13, w2)


def _route_plan(idx_t, n_experts, n_blocks):
    K, T = idx_t.shape
    e_flat = idx_t.T.reshape(T * K)
    sorted_e, sorted_m = lax.sort((e_flat, jnp.arange(T * K, dtype=jnp.int32)), num_keys=1, is_stable=True)
    start = jnp.searchsorted(sorted_e, jnp.arange(n_experts + 1, dtype=jnp.int32), side="left").astype(jnp.int32)
    counts = start[1:] - start[:-1]
    nblk = (counts + MOE_ROWS - 1) // MOE_ROWS
    bend = jnp.cumsum(nblk)
    j = jnp.arange(n_blocks, dtype=jnp.int32)
    used = j < bend[-1]
    be = jnp.minimum(jnp.searchsorted(bend, j, side="right"), n_experts - 1).astype(jnp.int32)
    off = (j - (bend - nblk)[be]) * MOE_ROWS
    nv = jnp.where(used, jnp.clip(counts[be] - off, 0, MOE_ROWS), 0).astype(jnp.int32)
    pos0 = jnp.where(used, start[be] + off, 0).astype(jnp.int32)
    return be, pos0, nv, sorted_m


def _combine_kernel(*refs, alpha):
    hf_ref, hp_ref, gate_ref = refs[:3]
    y_refs = refs[3:3 + TOP_K]
    w13_ref, w2_ref, lg_ref, lb_ref, of_ref, ob_ref = refs[3 + TOP_K:]
    D2 = hp_ref.shape[1]
    F = w2_ref.shape[0]
    xlo, xhi = _unpack_pair(hp_ref[...])
    h13 = (jnp.dot(xlo.astype(BF16), w13_ref[:D2, :], preferred_element_type=F32)
           + jnp.dot(xhi.astype(BF16), w13_ref[D2:, :], preferred_element_type=F32))
    h1 = h13[:, :F]
    hid = (h1 * _sigmoid(h1) * h13[:, F:]).astype(BF16)
    sh = jnp.dot(hid, w2_ref[...], preferred_element_type=F32)
    acc_lo = sh[:, :D2]
    acc_hi = sh[:, D2:]
    gate = gate_ref[...]
    for k in range(TOP_K):
        ylo, yhi = _unpack_pair(y_refs[k][...])
        gk = gate[:, k:k + 1]
        acc_lo = acc_lo + ylo * gk
        acc_hi = acc_hi + yhi * gk
    of_ref[:, :D2] = alpha * hf_ref[:, :D2] + acc_lo
    of_ref[:, D2:] = alpha * hf_ref[:, D2:] + acc_hi
    y = _layer_norm(of_ref[...], lg_ref[...], lb_ref[...])
    of_ref[...] = y
    ob_ref[...] = y.astype(BF16)


def _combine(hf, hp, gate, yk, w13, w2, lg, lb, alpha, tm=128):
    T, D = hf.shape
    D2 = D // 2
    F2 = w13.shape[1]
    nt = T // tm
    row = pl.BlockSpec((tm, D), lambda i: (i, 0))
    vecd = pl.BlockSpec((1, D), lambda i: (0, 0))
    y_specs = [pl.BlockSpec((tm, D2), functools.partial(lambda i, k: (k * nt + i, 0), k=k)) for k in range(TOP_K)]
    return pl.pallas_call(
        functools.partial(_combine_kernel, alpha=alpha), grid=(nt,),
        in_specs=[row, pl.BlockSpec((tm, D2), lambda i: (i, 0)), pl.BlockSpec((tm, TOP_K), lambda i: (i, 0))]
        + y_specs + [pl.BlockSpec((D, F2), lambda i: (0, 0)), pl.BlockSpec((F2 // 2, D), lambda i: (0, 0)), vecd, vecd],
        out_specs=[row, row],
        out_shape=[jax.ShapeDtypeStruct((T, D), F32), jax.ShapeDtypeStruct((T, D), BF16)],
        compiler_params=_params("arbitrary"), name="moe_combine_ln",
    )(hf, hp, gate, *([yk] * TOP_K), w13, w2, lg.reshape(1, D), lb.reshape(1, D))


def _rot_half_cols(w):
    half = w.shape[-1] // 2
    return jnp.concatenate([-w[..., half:], w[..., :half]], axis=-1)


def _pad_cols(w, n):
    return jnp.pad(w, ((0, 0), (0, n - w.shape[1])))


def kernel(x, mem, positions, ln_in_g, ln_in_b, w_in, q_norm_g, kv_norm_g, w_q_b, w_kv_b, conv_w, conv_b, w_rg_a, b_rg_a, w_rg_x, b_rg_x, lru_lambda, mla_out_g, rnn_out_g, w_o, ln1_g, ln1_b, w_xq, w_xk, w_xv, w_xo, ln2_g, ln2_b, w_router, router_bias, w_e1, w_e3, w_e2, w_s1, w_s3, w_s2, ln3_g, ln3_b):
    B, S, D = x.shape
    M = mem.shape[1]
    L = w_in.shape[0]
    T = B * S
    q_lora = q_norm_g.shape[1]
    kv_lora = kv_norm_g.shape[1]
    H, qk_dim = w_q_b.shape[2], w_q_b.shape[3]
    C = conv_b.shape[1]
    rope = w_in.shape[2] - q_lora - kv_lora - 2 * C
    nope = qk_dim - rope
    E = w_router.shape[2]
    assert nope == LANE and rope == LANE // 2 and w_kv_b.shape[3] - nope == LANE
    assert TOP_K & (TOP_K - 1) == 0
    alpha = (2.0 * L) ** 0.25
    n_blocks = -(-(T * TOP_K) // MOE_ROWS) + E

    off_kv, off_kr = q_lora, q_lora + kv_lora
    off_u, off_g = off_kr + rope, off_kr + rope + C
    col_kv, col_q, col_kr = 2 * C, 2 * C + kv_lora, 2 * C + kv_lora + q_lora
    assert col_kv % kv_lora == 0 and col_q % q_lora == 0 and col_kr % LANE == 0
    n_proj = -(-(col_kr + LANE) // 512) * 512

    cs = _rope_tables(positions, rope)
    hf, hb = _ln_in(x.reshape(T, D), ln_in_g, ln_in_b)
    memf = mem.reshape(B * M, D)

    for l in range(L):
        wi = w_in[l]
        w_kr = wi[:, off_kr:off_kr + rope]
        w_proj = _pad_cols(jnp.concatenate(
            [wi[:, off_u:off_u + C], wi[:, off_g:off_g + C], wi[:, off_kv:off_kv + kv_lora], wi[:, :q_lora],
             w_kr, _rot_half_cols(w_kr)], axis=1), n_proj).astype(BF16)
        wq3 = w_q_b[l]
        wq = jnp.concatenate([wq3[..., :nope], wq3[..., nope:], _rot_half_cols(wq3[..., nope:])],
                             axis=-1).reshape(q_lora, H * 2 * LANE).astype(BF16)
        wkv3 = w_kv_b[l]
        wkv = jnp.concatenate([wkv3[..., :nope].reshape(kv_lora, H * LANE),
                               wkv3[..., nope:].reshape(kv_lora, H * LANE)], axis=1).astype(BF16)

        proj = _matmul(hb, w_proj, tm=min(1024, T), tn=512, name="in_proj")
        q = _q_up(proj, col_q // q_lora, q_norm_g[l], cs, wq, qk_dim ** -0.5, tm=min(512, T))
        kv, kr = _kv_up(proj, col_kv // kv_lora, col_kr // LANE, kv_norm_g[l], cs, wkv, tm=min(512, T))
        attn = _attention(q, kv, kr, B, S, H)
        rec = _rglru(proj, B, S, C, conv_w[l].reshape(conv_w.shape[1], C), conv_b[l],
                     w_rg_a[l].astype(BF16), b_rg_a[l], w_rg_x[l].astype(BF16), b_rg_x[l], lru_lambda[l])
        hf, hb = _out_proj(attn, rec, mla_out_g[l], rnn_out_g[l], hf, w_o[l].astype(BF16), ln1_g[l], ln1_b[l], alpha)

        w_mkv = jnp.concatenate([w_xk[l], w_xv[l]], axis=1).astype(BF16)
        kvm = _matmul(memf, w_mkv, tm=min(256, B * M), tn=w_mkv.shape[1], name="mem_kv")
        hf, hp = _xattn(hb, hf, kvm, w_xq[l].astype(BF16), w_xo[l].astype(BF16), ln2_g[l], ln2_b[l], alpha, S, M)

        idx_t, gate_t = _router(hf, w_router[l], router_bias[l])
        be, pos0, nv, sorted_m = _route_plan(idx_t, E, n_blocks)
        w13 = jnp.concatenate([w_e1[l], w_e3[l]], axis=-1).astype(BF16)
        yk = _moe(hp, be, pos0, nv, sorted_m, w13, w_e2[l].astype(BF16), n_blocks)
        ws13 = jnp.concatenate([w_s1[l], w_s3[l]], axis=-1).astype(BF16)
        hf, hb = _combine(hf, hp, gate_t.T, yk, ws13, w_s2[l].astype(BF16), ln3_g[l], ln3_b[l], alpha)

    return hf.reshape(B, S, D)
```

```python
import functools
import math

import jax
import jax.numpy as jnp
from jax import lax
from jax.experimental import pallas as pl
from jax.experimental.pallas import tpu as pltpu

F32 = jnp.float32
BF16 = jnp.bfloat16
U32 = jnp.uint32

LANE = 128
VMEM_LIMIT = 56 * 1024 * 1024

ROPE_THETA = 10000.0
X_HEAD_DIM = 128
TOP_K = 8
N_GROUPS = 8
TOPK_GROUPS = 4
ROUTED_SCALE = 2.5
LRU_C = 8.0
LN_EPS = 1e-5
RMS_EPS = 1e-6
MOE_ROWS = 256


def _params(*sem):
    return pltpu.CompilerParams(dimension_semantics=sem, vmem_limit_bytes=VMEM_LIMIT)


def _layer_norm(y, g, b):
    mu = jnp.mean(y, -1, keepdims=True)
    yc = y - mu
    var = jnp.mean(yc * yc, -1, keepdims=True)
    return yc * lax.rsqrt(var + LN_EPS) * g + b


def _sigmoid(x):
    return 1.0 / (1.0 + jnp.exp(-x))


def _bits(x):
    return lax.bitcast_convert_type(x, U32)


def _pack_pair(lo, hi):
    lo_b = _bits(lo.astype(BF16).astype(F32))
    hi_b = _bits(hi.astype(BF16).astype(F32))
    return hi_b | lax.shift_right_logical(lo_b, jnp.uint32(16))


def _unpack_pair(w):
    lo = lax.bitcast_convert_type(lax.shift_left(w, jnp.uint32(16)), F32)
    hi = lax.bitcast_convert_type(w & jnp.uint32(0xFFFF0000), F32)
    return lo, hi


def _ln_in_kernel(x_ref, g_ref, b_ref, hf_ref, hb_ref):
    y = _layer_norm(x_ref[...], g_ref[...], b_ref[...])
    hf_ref[...] = y
    hb_ref[...] = y.astype(BF16)


def _ln_in(x, g, b, tm=256):
    T, D = x.shape
    row = pl.BlockSpec((tm, D), lambda i: (i, 0))
    vec = pl.BlockSpec((1, D), lambda i: (0, 0))
    return pl.pallas_call(
        _ln_in_kernel, grid=(T // tm,),
        in_specs=[row, vec, vec], out_specs=[row, row],
        out_shape=[jax.ShapeDtypeStruct((T, D), F32), jax.ShapeDtypeStruct((T, D), BF16)],
        compiler_params=_params("arbitrary"), name="ln_in",
    )(x, g.reshape(1, D), b.reshape(1, D))


def _mm_kernel(x_ref, w_ref, o_ref):
    o_ref[...] = jnp.dot(x_ref[...].astype(BF16), w_ref[...],
                         preferred_element_type=F32).astype(o_ref.dtype)


def _matmul(x, w, tm, tn, name):
    M, K = x.shape
    N = w.shape[1]
    return pl.pallas_call(
        _mm_kernel, grid=(M // tm, N // tn),
        in_specs=[pl.BlockSpec((tm, K), lambda i, j: (i, 0)), pl.BlockSpec((K, tn), lambda i, j: (0, j))],
        out_specs=pl.BlockSpec((tm, tn), lambda i, j: (i, j)),
        out_shape=jax.ShapeDtypeStruct((M, N), BF16),
        compiler_params=_params("arbitrary", "arbitrary"), name=name,
    )(x, w)


def _rope_kernel(pos_ref, freq_ref, cs_ref):
    ang = pos_ref[...].astype(F32) * freq_ref[...]
    keep = lax.broadcasted_iota(jnp.int32, ang.shape, 1) < LANE // 2
    cs_ref[:, :LANE] = jnp.where(keep, jnp.cos(ang), 0.0)
    cs_ref[:, LANE:] = jnp.where(keep, jnp.sin(ang), 0.0)


def _rope_tables(positions, rope_dim, tm=2048):
    T = positions.size
    half = rope_dim // 2
    freqs = ROPE_THETA ** (-jnp.arange(0, rope_dim, 2, dtype=F32) / rope_dim)
    freq_row = jnp.concatenate([freqs, freqs, jnp.zeros((LANE - 2 * half,), F32)]).reshape(1, LANE)
    tm = min(tm, T)
    return pl.pallas_call(
        _rope_kernel, grid=(T // tm,),
        in_specs=[pl.BlockSpec((tm, 1), lambda i: (i, 0)), pl.BlockSpec((1, LANE), lambda i: (0, 0))],
        out_specs=pl.BlockSpec((tm, 2 * LANE), lambda i: (i, 0)),
        out_shape=jax.ShapeDtypeStruct((T, 2 * LANE), F32),
        compiler_params=_params("arbitrary"), name="rope_tables",
    )(positions.reshape(T, 1), freq_row)


def _rope_lanes(v, cm, sm):
    return v * cm + pltpu.roll(v, LANE // 2, axis=1) * sm


def _q_up_kernel(x_ref, g_ref, cs_ref, w_ref, o_ref, xn_ref, *, scale):
    @pl.when(pl.program_id(1) == 0)
    def _():
        x = x_ref[...].astype(F32)
        r = lax.rsqrt(jnp.mean(x * x, -1, keepdims=True) + RMS_EPS)
        xn_ref[...] = (x * r * g_ref[...]).astype(BF16)

    y = jnp.dot(xn_ref[...], w_ref[...], preferred_element_type=F32)
    cm = cs_ref[:, :LANE] * scale
    sm = cs_ref[:, LANE:] * scale
    for h in range(y.shape[1] // (2 * LANE)):
        c0 = h * 2 * LANE
        o_ref[:, c0:c0 + LANE] = (y[:, c0:c0 + LANE] * scale).astype(BF16)
        o_ref[:, c0 + LANE:c0 + 2 * LANE] = _rope_lanes(y[:, c0 + LANE:c0 + 2 * LANE], cm, sm).astype(BF16)


def _q_up(proj, col_block, q_norm_g, cs, wq, scale, tm=512, tn=1024):
    T = proj.shape[0]
    R, N = wq.shape
    tn = min(tn, N)
    return pl.pallas_call(
        functools.partial(_q_up_kernel, scale=scale), grid=(T // tm, N // tn),
        in_specs=[pl.BlockSpec((tm, R), lambda i, j: (i, col_block)),
                  pl.BlockSpec((1, R), lambda i, j: (0, 0)),
                  pl.BlockSpec((tm, 2 * LANE), lambda i, j: (i, 0)),
                  pl.BlockSpec((R, tn), lambda i, j: (0, j))],
        out_specs=pl.BlockSpec((tm, tn), lambda i, j: (i, j)),
        out_shape=jax.ShapeDtypeStruct((T, N), BF16),
        scratch_shapes=[pltpu.VMEM((tm, R), BF16)],
        compiler_params=_params("arbitrary", "arbitrary"), name="q_up",
    )(proj, q_norm_g.reshape(1, R), cs, wq)


def _kv_up_kernel(x_ref, g_ref, kr_ref, cs_ref, w_ref, o_ref, kro_ref, xn_ref):
    @pl.when(pl.program_id(1) == 0)
    def _():
        x = x_ref[...].astype(F32)
        r = lax.rsqrt(jnp.mean(x * x, -1, keepdims=True) + RMS_EPS)
        xn_ref[...] = (x * r * g_ref[...]).astype(BF16)
        kro_ref[...] = _rope_lanes(kr_ref[...].astype(F32), cs_ref[:, :LANE], cs_ref[:, LANE:]).astype(BF16)

    o_ref[...] = jnp.dot(xn_ref[...], w_ref[...], preferred_element_type=F32).astype(BF16)


def _kv_up(proj, kv_block, kr_block, kv_norm_g, cs, wkv, tm=512, tn=1024):
    T = proj.shape[0]
    R, N = wkv.shape
    tn = min(tn, N)
    return pl.pallas_call(
        _kv_up_kernel, grid=(T // tm, N // tn),
        in_specs=[pl.BlockSpec((tm, R), lambda i, j: (i, kv_block)),
                  pl.BlockSpec((1, R), lambda i, j: (0, 0)),
                  pl.BlockSpec((tm, LANE), lambda i, j: (i, kr_block)),
                  pl.BlockSpec((tm, 2 * LANE), lambda i, j: (i, 0)),
                  pl.BlockSpec((R, tn), lambda i, j: (0, j))],
        out_specs=[pl.BlockSpec((tm, tn), lambda i, j: (i, j)),
                   pl.BlockSpec((tm, LANE), lambda i, j: (i, 0))],
        out_shape=[jax.ShapeDtypeStruct((T, N), BF16), jax.ShapeDtypeStruct((T, LANE), BF16)],
        scratch_shapes=[pltpu.VMEM((tm, R), BF16)],
        compiler_params=_params("arbitrary", "arbitrary"), name="kv_up",
    )(proj, kv_norm_g.reshape(1, R), proj, cs, wkv)


def _attn_kernel(q_ref, kn_ref, kr_ref, v_ref, o_ref, k_ref, *, tq):
    S = q_ref.shape[0]
    k_ref[:, :LANE] = kn_ref[...]
    k_ref[:, LANE:] = kr_ref[...]
    nt = (((1,), (1,)), ((), ()))
    row = lax.broadcasted_iota(jnp.int32, (tq, tq), 0)
    col = lax.broadcasted_iota(jnp.int32, (tq, tq), 1)
    tri = row >= col
    for qi in range(S // tq):
        lo = qi * tq
        q = q_ref[lo:lo + tq, :]
        s_d = lax.dot_general(q, k_ref[lo:lo + tq, :], nt, preferred_element_type=F32)
        s_d = jnp.where(tri, s_d, -jnp.inf)
        m = jnp.max(s_d, -1, keepdims=True)
        if qi > 0:
            s_o = lax.dot_general(q, k_ref[:lo, :], nt, preferred_element_type=F32)
            m = jnp.maximum(m, jnp.max(s_o, -1, keepdims=True))
        p_d = jnp.exp(s_d - m)
        l = jnp.sum(p_d, -1, keepdims=True)
        acc = jnp.dot(p_d.astype(BF16), v_ref[lo:lo + tq, :], preferred_element_type=F32)
        if qi > 0:
            p_o = jnp.exp(s_o - m)
            l = l + jnp.sum(p_o, -1, keepdims=True)
            acc = acc + jnp.dot(p_o.astype(BF16), v_ref[:lo, :], preferred_element_type=F32)
        o_ref[lo:lo + tq, :] = (acc / l).astype(BF16)


def _attention(q, kv, kr, B, S, H, tq=256):
    T = B * S
    tq = min(tq, S)
    return pl.pallas_call(
        functools.partial(_attn_kernel, tq=tq), grid=(B, H),
        in_specs=[pl.BlockSpec((S, 2 * LANE), lambda b, h: (b, h)),
                  pl.BlockSpec((S, LANE), lambda b, h: (b, h)),
                  pl.BlockSpec((S, LANE), lambda b, h: (b, 0)),
                  pl.BlockSpec((S, LANE), lambda b, h: (b, H + h))],
        out_specs=pl.BlockSpec((S, LANE), lambda b, h: (b, h)),
        out_shape=jax.ShapeDtypeStruct((T, H * LANE), BF16),
        scratch_shapes=[pltpu.VMEM((S, 2 * LANE), BF16)],
        compiler_params=_params("arbitrary", "arbitrary"), name="mla_attention",
    )(q, kv, kr, kv)


def _rglru_kernel(u_ref, g_ref, cw_ref, cb_ref, wa_ref, ba_ref, wx_ref, bx_ref, lam_ref, o_ref,
                  a_ref, b_ref):
    S, tc = u_ref.shape
    u = u_ref[...].astype(F32)
    row = lax.broadcasted_iota(jnp.int32, (S, tc), 0)
    cw = cw_ref[...]
    nconv = cw.shape[0]
    xc = u * cw[nconv - 1:nconv, :] + cb_ref[...]
    for d in range(1, nconv):
        xc = xc + jnp.where(row >= d, pltpu.roll(u, d, axis=0), 0.0) * cw[nconv - 1 - d:nconv - d, :]
    lam = lam_ref[...]
    z = -lam
    softplus = jnp.maximum(z, 0.0) + jnp.log1p(jnp.exp(-jnp.abs(z)))
    for blk in range(tc // LANE):
        sl = slice(blk * LANE, (blk + 1) * LANE)
        xb = xc[:, sl]
        xbb = xb.astype(BF16)
        r = _sigmoid(jnp.dot(xbb, wa_ref[blk], preferred_element_type=F32) + ba_ref[:, sl])
        gi = _sigmoid(jnp.dot(xbb, wx_ref[blk], preferred_element_type=F32) + bx_ref[:, sl])
        log_a = (-LRU_C) * r * softplus[:, sl]
        a_blk = jnp.exp(log_a)
        a_ref[:, sl] = a_blk
        b_ref[:, sl] = jnp.sqrt(-jnp.tanh(log_a) * (a_blk * a_blk + 1.0)) * (gi * xb)

    a = a_ref[...]
    b = b_ref[...]
    sub = row & 7
    for d in (1, 2, 4):
        ok = sub >= d
        b = b + a * jnp.where(ok, pltpu.roll(b, d, axis=0), 0.0)
        a = a * jnp.where(ok, pltpu.roll(a, d, axis=0), 1.0)
    a_ref[...] = a
    b_ref[...] = b

    def carry(t, hprev):
        r0 = pl.multiple_of(t * 8, 8)
        h = b_ref[pl.ds(r0, 8), :] + a_ref[pl.ds(r0, 8), :] * hprev
        b_ref[pl.ds(r0, 8), :] = h
        return jnp.broadcast_to(h[7:8, :], h.shape)

    lax.fori_loop(0, S // 8, carry, jnp.zeros((8, tc), F32))
    gg = g_ref[...].astype(F32)
    gelu = 0.5 * gg * (1.0 + jnp.tanh(math.sqrt(2.0 / math.pi) * (gg + 0.044715 * (gg * gg * gg))))
    o_ref[...] = (b_ref[...] * gelu).astype(BF16)


def _rglru(proj, B, S, C, conv_w, conv_b, wa, ba, wx, bx, lam, tc=256):
    T = B * S
    nb = tc // LANE
    ncb = C // tc
    vec = pl.BlockSpec((1, tc), lambda b, j: (0, j))
    wspec = pl.BlockSpec((nb, LANE, LANE), lambda b, j: (j, 0, 0))
    nconv = conv_w.shape[0]
    return pl.pallas_call(
        _rglru_kernel, grid=(B, ncb),
        in_specs=[pl.BlockSpec((S, tc), lambda b, j: (b, j)),
                  pl.BlockSpec((S, tc), lambda b, j: (b, ncb + j)),
                  pl.BlockSpec((nconv, tc), lambda b, j: (0, j)),
                  vec, wspec, vec, wspec, vec, vec],
        out_specs=pl.BlockSpec((S, tc), lambda b, j: (b, j)),
        out_shape=jax.ShapeDtypeStruct((T, C), BF16),
        scratch_shapes=[pltpu.VMEM((S, tc), F32), pltpu.VMEM((S, tc), F32)],
        compiler_params=_params("arbitrary", "arbitrary"), name="rglru",
    )(proj, proj, conv_w, conv_b.reshape(1, C), wa, ba.reshape(1, C), wx, bx.reshape(1, C),
      lam.reshape(1, C))


def _out_proj_kernel(a_ref, r_ref, ga_ref, gr_ref, hf_ref, w_ref, lg_ref, lb_ref, of_ref, ob_ref,
                     xs_ref, acc_ref, *, alpha):
    j = pl.program_id(1)
    nj = pl.num_programs(1)
    W = a_ref.shape[1]

    @pl.when(j == 0)
    def _():
        for src, g, c0 in ((a_ref, ga_ref, 0), (r_ref, gr_ref, W)):
            x = src[...].astype(F32)
            r = lax.rsqrt(jnp.mean(x * x, -1, keepdims=True) + RMS_EPS)
            xs_ref[:, c0:c0 + W] = (x * r * g[...]).astype(BF16)

    acc_ref[j] = jnp.dot(xs_ref[...], w_ref[...], preferred_element_type=F32)

    @pl.when(j == nj - 1)
    def _():
        tn = acc_ref.shape[2]
        for c in range(acc_ref.shape[0]):
            of_ref[:, c * tn:(c + 1) * tn] = alpha * hf_ref[:, c * tn:(c + 1) * tn] + acc_ref[c]
        y = _layer_norm(of_ref[...], lg_ref[...], lb_ref[...])
        of_ref[...] = y
        ob_ref[...] = y.astype(BF16)


def _out_proj(attn, rec, ga, gr, hf, w_o, lg, lb, alpha, tm=256, tn=512):
    T, W = attn.shape
    D = w_o.shape[1]
    tn = min(tn, D)
    row = pl.BlockSpec((tm, D), lambda i, j: (i, 0))
    half = pl.BlockSpec((tm, W), lambda i, j: (i, 0))
    vecw = pl.BlockSpec((1, W), lambda i, j: (0, 0))
    vecd = pl.BlockSpec((1, D), lambda i, j: (0, 0))
    return pl.pallas_call(
        functools.partial(_out_proj_kernel, alpha=alpha), grid=(T // tm, D // tn),
        in_specs=[half, half, vecw, vecw, row, pl.BlockSpec((2 * W, tn), lambda i, j: (0, j)), vecd, vecd],
        out_specs=[row, row],
        out_shape=[jax.ShapeDtypeStruct((T, D), F32), jax.ShapeDtypeStruct((T, D), BF16)],
        scratch_shapes=[pltpu.VMEM((tm, 2 * W), BF16), pltpu.VMEM((D // tn, tm, tn), F32)],
        compiler_params=_params("arbitrary", "arbitrary"), name="out_proj_ln",
    )(attn, rec, ga.reshape(1, W), gr.reshape(1, W), hf, w_o, lg.reshape(1, D), lb.reshape(1, D))


def _xattn_kernel(hb_ref, hf_ref, kv_ref, wq_ref, wo_ref, lg_ref, lb_ref, of_ref, op_ref, *, alpha, scale):
    XW = wq_ref.shape[1]
    D = hf_ref.shape[1]
    q = (jnp.dot(hb_ref[...], wq_ref[...], preferred_element_type=F32) * scale).astype(BF16)
    nt = (((1,), (1,)), ((), ()))
    outs = []
    for h in range(XW // X_HEAD_DIM):
        sl = slice(h * X_HEAD_DIM, (h + 1) * X_HEAD_DIM)
        s = lax.dot_general(q[:, sl], kv_ref[:, sl], nt, preferred_element_type=F32)
        m = jnp.max(s, -1, keepdims=True)
        p = jnp.exp(s - m)
        l = jnp.sum(p, -1, keepdims=True)
        o = jnp.dot(p.astype(BF16), kv_ref[:, XW + h * X_HEAD_DIM:XW + (h + 1) * X_HEAD_DIM],
                    preferred_element_type=F32)
        outs.append((o / l).astype(BF16))
    o = jnp.concatenate(outs, axis=-1)
    y = alpha * hf_ref[...] + jnp.dot(o, wo_ref[...], preferred_element_type=F32)
    y = _layer_norm(y, lg_ref[...], lb_ref[...])
    of_ref[...] = y
    op_ref[...] = _pack_pair(y[:, :D // 2], y[:, D // 2:])


def _xattn(hb, hf, kvm, wq, wo, lg, lb, alpha, S, M, tm=256):
    T, D = hf.shape
    XW = wq.shape[1]
    per_b = S // tm
    row = pl.BlockSpec((tm, D), lambda i: (i, 0))
    vecd = pl.BlockSpec((1, D), lambda i: (0, 0))
    return pl.pallas_call(
        functools.partial(_xattn_kernel, alpha=alpha, scale=X_HEAD_DIM ** -0.5), grid=(T // tm,),
        in_specs=[row, row, pl.BlockSpec((M, 2 * XW), lambda i: (i // per_b, 0)),
                  pl.BlockSpec((D, XW), lambda i: (0, 0)), pl.BlockSpec((XW, D), lambda i: (0, 0)),
                  vecd, vecd],
        out_specs=[row, pl.BlockSpec((tm, D // 2), lambda i: (i, 0))],
        out_shape=[jax.ShapeDtypeStruct((T, D), F32), jax.ShapeDtypeStruct((T, D // 2), U32)],
        compiler_params=_params("arbitrary"), name="mem_xattn_ln",
    )(hb, hf, kvm, wq, wo, lg.reshape(1, D), lb.reshape(1, D))


def _router_kernel(hf_ref, whi_ref, wlo_ref, bias_ref, idx_ref, gate_ref):
    x = hf_ref[...]
    xh = x.astype(BF16)
    xl = (x - xh.astype(F32)).astype(BF16)
    whi = whi_ref[...]
    logits = (jnp.dot(xh, whi, preferred_element_type=F32)
              + (jnp.dot(xl, whi, preferred_element_type=F32)
                 + jnp.dot(xh, wlo_ref[...], preferred_element_type=F32)))
    E = bias_ref.shape[0]
    tm = x.shape[0]
    per = E // N_GROUPS
    scores = _sigmoid(logits.T[:E, :])
    biased = scores + bias_ref[...]
    neg = -jnp.inf
    b3 = biased.reshape(N_GROUPS, per, tm)
    i3 = lax.broadcasted_iota(jnp.int32, b3.shape, 1)
    m1 = jnp.max(b3, axis=1, keepdims=True)
    first = jnp.min(jnp.where(b3 == m1, i3, per), axis=1, keepdims=True)
    m2 = jnp.max(jnp.where(i3 == first, neg, b3), axis=1, keepdims=True)
    gscore = (m1 + m2).reshape(N_GROUPS, tm)
    gi = lax.broadcasted_iota(jnp.int32, gscore.shape, 0)
    gsel = jnp.zeros(gscore.shape, F32)
    work = gscore
    for _ in range(TOPK_GROUPS):
        m = jnp.max(work, axis=0, keepdims=True)
        pick = jnp.min(jnp.where(work == m, gi, N_GROUPS), axis=0, keepdims=True)
        hit = gi == pick
        gsel = jnp.where(hit, 1.0, gsel)
        work = jnp.where(hit, neg, work)
    keep = jnp.broadcast_to(gsel.reshape(N_GROUPS, 1, tm), b3.shape).reshape(E, tm)
    masked = jnp.where(keep > 0.0, biased, neg)
    ei = lax.broadcasted_iota(jnp.int32, masked.shape, 0)
    ids, gates = [], []
    for _ in range(TOP_K):
        m = jnp.max(masked, axis=0, keepdims=True)
        pick = jnp.min(jnp.where(masked == m, ei, E), axis=0, keepdims=True)
        hit = ei == pick
        ids.append(pick)
        gates.append(jnp.sum(jnp.where(hit, scores, 0.0), axis=0, keepdims=True))
        masked = jnp.where(hit, neg, masked)
    g = jnp.concatenate(gates, axis=0)
    idx_ref[...] = jnp.concatenate(ids, axis=0)
    gate_ref[...] = g / jnp.sum(g, axis=0, keepdims=True) * ROUTED_SCALE


def _router(hf, w_router, bias, tm=256):
    T, D = hf.shape
    E = w_router.shape[1]
    wpad = jnp.pad(w_router, ((0, 0), (0, LANE - E)))
    whi = wpad.astype(BF16)
    wlo = (wpad - whi.astype(F32)).astype(BF16)
    wspec = pl.BlockSpec((D, LANE), lambda i: (0, 0))
    ospec = pl.BlockSpec((TOP_K, tm), lambda i: (0, i))
    return pl.pallas_call(
        _router_kernel, grid=(T // tm,),
        in_specs=[pl.BlockSpec((tm, D), lambda i: (i, 0)), wspec, wspec, pl.BlockSpec((E, 1), lambda i: (0, 0))],
        out_specs=[ospec, ospec],
        out_shape=[jax.ShapeDtypeStruct((TOP_K, T), jnp.int32), jax.ShapeDtypeStruct((TOP_K, T), F32)],
        compiler_params=_params("arbitrary"), name="router",
    )(hf, whi, wlo, bias.reshape(E, 1))


def _moe_kernel(be_ref, pos0_ref, nv_ref, sm_ref, x_hbm, w1_ref, w3_ref, w2_ref, y_hbm, xbuf, ybuf, gsem, ssem,
                *, n_tok, n_chunks):
    i = pl.program_id(0)
    nb = pl.num_programs(0)
    slot = lax.rem(i, 2)
    other = 1 - slot
    R, D2 = xbuf.shape[1], xbuf.shape[2]
    F = w2_ref.shape[0]
    shift = TOP_K.bit_length() - 1
    last_entry = TOP_K * n_tok - 1
    cw = D2 // n_chunks
    per = R // (2 * n_chunks)

    nxt = jnp.minimum(i + 1, nb - 1)
    prv = jnp.maximum(i - 1, 0)
    g_pos0 = pos0_ref[nxt]
    s_pos0 = pos0_ref[prv]
    s_nv = jnp.where(i > 0, nv_ref[prv], 0)
    spare0 = TOP_K * n_tok + other * R

    def gather_rows(pos0, sl, r0, r1):
        for r in range(r0, r1):
            m = sm_ref[jnp.minimum(pos0 + r, last_entry)]
            tok = lax.shift_right_logical(m, shift)
            pltpu.make_async_copy(x_hbm.at[pl.ds(tok, 1), :], xbuf.at[sl, pl.ds(r, 1), :], gsem.at[sl]).start()

    def gather_wait(sl):
        for r in range(R):
            pltpu.make_async_copy(x_hbm.at[pl.ds(0, 1), :], xbuf.at[sl, pl.ds(r, 1), :], gsem.at[sl]).wait()

    def scatter_rows(pos0, nv, sl, spare, r0, r1):
        for r in range(r0, r1):
            m = sm_ref[jnp.minimum(pos0 + r, last_entry)]
            dest = (m & (TOP_K - 1)) * n_tok + lax.shift_right_logical(m, shift)
            dest = jnp.where(r < nv, dest, spare + r)
            pltpu.make_async_copy(ybuf.at[sl, pl.ds(r, 1), :], y_hbm.at[pl.ds(dest, 1), :], ssem.at[sl]).start()

    def scatter_wait(sl):
        for r in range(R):
            pltpu.make_async_copy(ybuf.at[sl, pl.ds(r, 1), :], y_hbm.at[pl.ds(0, 1), :], ssem.at[sl]).wait()

    @pl.when(i == 0)
    def _():
        ybuf[...] = jnp.zeros(ybuf.shape, U32)
        gather_rows(pos0_ref[0], 0, 0, R)

    gather_wait(slot)

    def issue(c):
        gather_rows(g_pos0, other, c * per, (c + 1) * per)
        scatter_rows(s_pos0, s_nv, other, spare0, c * per, (c + 1) * per)

    acc = None
    for c in range(n_chunks):
        issue(c)
        lo, hi = _unpack_pair(xbuf[slot, :, c * cw:(c + 1) * cw])
        xc = jnp.concatenate([lo.astype(BF16), hi.astype(BF16)], axis=1)
        ra = slice(c * cw, (c + 1) * cw)
        rb = slice(D2 + c * cw, D2 + (c + 1) * cw)
        wc = jnp.concatenate([jnp.concatenate([w1_ref[ra, :], w3_ref[ra, :]], axis=1),
                              jnp.concatenate([w1_ref[rb, :], w3_ref[rb, :]], axis=1)], axis=0).astype(BF16)
        part = jnp.dot(xc, wc, preferred_element_type=F32)
        acc = part if acc is None else acc + part
    h1 = acc[:, :F]
    hid = (h1 * _sigmoid(h1) * acc[:, F:]).astype(BF16)

    @pl.when(i >= 1)
    def _():
        scatter_wait(slot)

    for c in range(n_chunks):
        issue(n_chunks + c)
        wc = jnp.concatenate([w2_ref[:, c * cw:(c + 1) * cw], w2_ref[:, D2 + c * cw:D2 + (c + 1) * cw]],
                             axis=1).astype(BF16)
        yc = jnp.dot(hid, wc, preferred_element_type=F32)
        ybuf[slot, :, c * cw:(c + 1) * cw] = _pack_pair(yc[:, :cw], yc[:, cw:])

    @pl.when(i == nb - 1)
    def _():
        scatter_rows(pos0_ref[i], nv_ref[i], slot, TOP_K * n_tok + slot * R, 0, R)
        gather_wait(other)
        scatter_wait(other)
        scatter_wait(slot)


def _moe(hp, block_e, block_pos0, block_nv, sorted_m, w_e1, w_e3, w_e2, layer, n_blocks):
    T, D2 = hp.shape
    _, E, D, F = w_e1.shape
    n_chunks = max(1, min(8, D2 // (2 * LANE)))
    wmap = lambda i, be, p0, nv, sm: (layer, be[i], 0, 0)
    grid_spec = pltpu.PrefetchScalarGridSpec(
        num_scalar_prefetch=4, grid=(n_blocks,),
        in_specs=[pl.BlockSpec(memory_space=pl.ANY),
                  pl.BlockSpec((None, None, D, F), wmap),
                  pl.BlockSpec((None, None, D, F), wmap),
                  pl.BlockSpec((None, None, F, D), wmap)],
        out_specs=pl.BlockSpec(memory_space=pl.ANY),
        scratch_shapes=[pltpu.VMEM((2, MOE_ROWS, D2), U32), pltpu.VMEM((2, MOE_ROWS, D2), U32),
                        pltpu.SemaphoreType.DMA((2,)), pltpu.SemaphoreType.DMA((2,))])
    return pl.pallas_call(
        functools.partial(_moe_kernel, n_tok=T, n_chunks=n_chunks), grid_spec=grid_spec,
        out_shape=jax.ShapeDtypeStruct((TOP_K * T + 2 * MOE_ROWS, D2), U32),
        compiler_params=_params("arbitrary"), name="moe_experts",
    )(block_e, block_pos0, block_nv, sorted_m, hp, w_e1, w_e3, w_e2)


def _route_plan(idx_t, n_experts, n_blocks):
    K, T = idx_t.shape
    e_flat = idx_t.T.reshape(T * K)
    sorted_e, sorted_m = lax.sort((e_flat, jnp.arange(T * K, dtype=jnp.int32)), num_keys=1, is_stable=True)
    edges = jnp.arange(n_experts + 1, dtype=jnp.int32)
    start = jnp.sum((sorted_e[None, :] < edges[:, None]).astype(jnp.int32), axis=1)
    counts = start[1:] - start[:-1]
    nblk = (counts + MOE_ROWS - 1) // MOE_ROWS
    bend = jnp.cumsum(nblk)
    j = jnp.arange(n_blocks, dtype=jnp.int32)
    used = j < bend[-1]
    be = jnp.minimum(jnp.sum((bend[None, :] <= j[:, None]).astype(jnp.int32), axis=1), n_experts - 1)
    off = (j - (bend - nblk)[be]) * MOE_ROWS
    nv = jnp.where(used, jnp.clip(counts[be] - off, 0, MOE_ROWS), 0).astype(jnp.int32)
    pos0 = jnp.where(used, start[be] + off, 0).astype(jnp.int32)
    return be, pos0, nv, sorted_m


def _combine_kernel(*refs, alpha):
    hf_ref, hp_ref, gate_ref = refs[:3]
    y_refs = refs[3:3 + TOP_K]
    w13_ref, w2_ref, lg_ref, lb_ref, of_ref, ob_ref = refs[3 + TOP_K:]
    D2 = hp_ref.shape[1]
    F = w2_ref.shape[0]
    xlo, xhi = _unpack_pair(hp_ref[...])
    h13 = (jnp.dot(xlo.astype(BF16), w13_ref[:D2, :], preferred_element_type=F32)
           + jnp.dot(xhi.astype(BF16), w13_ref[D2:, :], preferred_element_type=F32))
    h1 = h13[:, :F]
    hid = (h1 * _sigmoid(h1) * h13[:, F:]).astype(BF16)
    sh = jnp.dot(hid, w2_ref[...], preferred_element_type=F32)
    acc_lo = sh[:, :D2]
    acc_hi = sh[:, D2:]
    gate = gate_ref[...]
    for k in range(TOP_K):
        ylo, yhi = _unpack_pair(y_refs[k][...])
        gk = gate[:, k:k + 1]
        acc_lo = acc_lo + ylo * gk
        acc_hi = acc_hi + yhi * gk
    of_ref[:, :D2] = alpha * hf_ref[:, :D2] + acc_lo
    of_ref[:, D2:] = alpha * hf_ref[:, D2:] + acc_hi
    y = _layer_norm(of_ref[...], lg_ref[...], lb_ref[...])
    of_ref[...] = y
    ob_ref[...] = y.astype(BF16)


def _combine(hf, hp, gate, yk, w13, w2, lg, lb, alpha, tm=128):
    T, D = hf.shape
    D2 = D // 2
    F2 = w13.shape[1]
    nt = T // tm
    row = pl.BlockSpec((tm, D), lambda i: (i, 0))
    vecd = pl.BlockSpec((1, D), lambda i: (0, 0))
    y_specs = [pl.BlockSpec((tm, D2), functools.partial(lambda i, k: (k * nt + i, 0), k=k)) for k in range(TOP_K)]
    return pl.pallas_call(
        functools.partial(_combine_kernel, alpha=alpha), grid=(nt,),
        in_specs=[row, pl.BlockSpec((tm, D2), lambda i: (i, 0)), pl.BlockSpec((tm, TOP_K), lambda i: (i, 0))]
        + y_specs + [pl.BlockSpec((D, F2), lambda i: (0, 0)), pl.BlockSpec((F2 // 2, D), lambda i: (0, 0)), vecd, vecd],
        out_specs=[row, row],
        out_shape=[jax.ShapeDtypeStruct((T, D), F32), jax.ShapeDtypeStruct((T, D), BF16)],
        compiler_params=_params("arbitrary"), name="moe_combine_ln",
    )(hf, hp, gate, *([yk] * TOP_K), w13, w2, lg.reshape(1, D), lb.reshape(1, D))


def _rot_half_cols(w):
    half = w.shape[-1] // 2
    return jnp.concatenate([-w[..., half:], w[..., :half]], axis=-1)


def _pad_cols(w, n):
    return jnp.pad(w, ((0, 0), (0, n - w.shape[1])))


def kernel(x, mem, positions, ln_in_g, ln_in_b, w_in, q_norm_g, kv_norm_g, w_q_b, w_kv_b, conv_w, conv_b, w_rg_a, b_rg_a, w_rg_x, b_rg_x, lru_lambda, mla_out_g, rnn_out_g, w_o, ln1_g, ln1_b, w_xq, w_xk, w_xv, w_xo, ln2_g, ln2_b, w_router, router_bias, w_e1, w_e3, w_e2, w_s1, w_s3, w_s2, ln3_g, ln3_b):
    B, S, D = x.shape
    M = mem.shape[1]
    L = w_in.shape[0]
    T = B * S
    q_lora = q_norm_g.shape[1]
    kv_lora = kv_norm_g.shape[1]
    H, qk_dim = w_q_b.shape[2], w_q_b.shape[3]
    C = conv_b.shape[1]
    rope = w_in.shape[2] - q_lora - kv_lora - 2 * C
    nope = qk_dim - rope
    E = w_router.shape[2]
    assert nope == LANE and rope == LANE // 2 and w_kv_b.shape[3] - nope == LANE
    assert TOP_K & (TOP_K - 1) == 0
    alpha = (2.0 * L) ** 0.25
    n_blocks = -(-(T * TOP_K) // MOE_ROWS) + E

    off_kv, off_kr = q_lora, q_lora + kv_lora
    off_u, off_g = off_kr + rope, off_kr + rope + C
    col_kv, col_q, col_kr = 2 * C, 2 * C + kv_lora, 2 * C + kv_lora + q_lora
    assert col_kv % kv_lora == 0 and col_q % q_lora == 0 and col_kr % LANE == 0
    n_proj = -(-(col_kr + LANE) // 512) * 512

    cs = _rope_tables(positions, rope)
    hf, hb = _ln_in(x.reshape(T, D), ln_in_g, ln_in_b)
    memf = mem.reshape(B * M, D)

    for l in range(L):
        wi = w_in[l]
        w_kr = wi[:, off_kr:off_kr + rope]
        w_proj = _pad_cols(jnp.concatenate(
            [wi[:, off_u:off_u + C], wi[:, off_g:off_g + C], wi[:, off_kv:off_kv + kv_lora], wi[:, :q_lora],
             w_kr, _rot_half_cols(w_kr)], axis=1), n_proj).astype(BF16)
        wq3 = w_q_b[l]
        wq = jnp.concatenate([wq3[..., :nope], wq3[..., nope:], _rot_half_cols(wq3[..., nope:])],
                             axis=-1).reshape(q_lora, H * 2 * LANE).astype(BF16)
        wkv3 = w_kv_b[l]
        wkv = jnp.concatenate([wkv3[..., :nope].reshape(kv_lora, H * LANE),
                               wkv3[..., nope:].reshape(kv_lora, H * LANE)], axis=1).astype(BF16)

        proj = _matmul(hb, w_proj, tm=min(1024, T), tn=512, name="in_proj")
        q = _q_up(proj, col_q // q_lora, q_norm_g[l], cs, wq, qk_dim ** -0.5, tm=min(512, T))
        kv, kr = _kv_up(proj, col_kv // kv_lora, col_kr // LANE, kv_norm_g[l], cs, wkv, tm=min(512, T))
        attn = _attention(q, kv, kr, B, S, H)
        rec = _rglru(proj, B, S, C, conv_w[l].reshape(conv_w.shape[1], C), conv_b[l],
                     w_rg_a[l].astype(BF16), b_rg_a[l], w_rg_x[l].astype(BF16), b_rg_x[l], lru_lambda[l])
        hf, hb = _out_proj(attn, rec, mla_out_g[l], rnn_out_g[l], hf, w_o[l].astype(BF16), ln1_g[l], ln1_b[l], alpha)

        w_mkv = jnp.concatenate([w_xk[l], w_xv[l]], axis=1).astype(BF16)
        kvm = _matmul(memf, w_mkv, tm=min(256, B * M), tn=w_mkv.shape[1], name="mem_kv")
        hf, hp = _xattn(hb, hf, kvm, w_xq[l].astype(BF16), w_xo[l].astype(BF16), ln2_g[l], ln2_b[l], alpha, S, M)

        idx_t, gate_t = _router(hf, w_router[l], router_bias[l])
        be, pos0, nv, sorted_m = _route_plan(idx_t, E, n_blocks)
        yk = _moe(hp, be, pos0, nv, sorted_m, w_e1, w_e3, w_e2, l, n_blocks)
        ws13 = jnp.concatenate([w_s1[l], w_s3[l]], axis=-1).astype(BF16)
        hf, hb = _combine(hf, hp, gate_t.T, yk, ws13, w_s2[l].astype(BF16), ln3_g[l], ln3_b[l], alpha)

    return hf.reshape(B, S, D)
```

```python
import functools
import math

import jax
import jax.numpy as jnp
from jax import lax
from jax.experimental import pallas as pl
from jax.experimental.pallas import tpu as pltpu

F32 = jnp.float32
BF16 = jnp.bfloat16
U32 = jnp.uint32

LANE = 128
VMEM_LIMIT = 56 * 1024 * 1024

ROPE_THETA = 10000.0
X_HEAD_DIM = 128
TOP_K = 8
N_GROUPS = 8
TOPK_GROUPS = 4
ROUTED_SCALE = 2.5
LRU_C = 8.0
LN_EPS = 1e-5
RMS_EPS = 1e-6
MOE_ROWS = 256


def _params(*sem):
    return pltpu.CompilerParams(dimension_semantics=sem, vmem_limit_bytes=VMEM_LIMIT)


def _layer_norm(y, g, b):
    mu = jnp.mean(y, -1, keepdims=True)
    yc = y - mu
    var = jnp.mean(yc * yc, -1, keepdims=True)
    return yc * lax.rsqrt(var + LN_EPS) * g + b


def _sigmoid(x):
    return 1.0 / (1.0 + jnp.exp(-x))


def _bits(x):
    return lax.bitcast_convert_type(x, U32)


def _pack_pair(lo, hi):
    lo_b = _bits(lo.astype(BF16).astype(F32))
    hi_b = _bits(hi.astype(BF16).astype(F32))
    return hi_b | lax.shift_right_logical(lo_b, jnp.uint32(16))


def _unpack_pair(w):
    lo = lax.bitcast_convert_type(lax.shift_left(w, jnp.uint32(16)), F32)
    hi = lax.bitcast_convert_type(w & jnp.uint32(0xFFFF0000), F32)
    return lo, hi


def _ln_in_kernel(x_ref, g_ref, b_ref, hf_ref, hb_ref):
    y = _layer_norm(x_ref[...], g_ref[...], b_ref[...])
    hf_ref[...] = y
    hb_ref[...] = y.astype(BF16)


def _ln_in(x, g, b, tm=256):
    T, D = x.shape
    row = pl.BlockSpec((tm, D), lambda i: (i, 0))
    vec = pl.BlockSpec((1, D), lambda i: (0, 0))
    return pl.pallas_call(
        _ln_in_kernel, grid=(T // tm,),
        in_specs=[row, vec, vec], out_specs=[row, row],
        out_shape=[jax.ShapeDtypeStruct((T, D), F32), jax.ShapeDtypeStruct((T, D), BF16)],
        compiler_params=_params("arbitrary"), name="ln_in",
    )(x, g.reshape(1, D), b.reshape(1, D))


def _mm_kernel(x_ref, w_ref, o_ref):
    o_ref[...] = jnp.dot(x_ref[...].astype(BF16), w_ref[...],
                         preferred_element_type=F32).astype(o_ref.dtype)


def _matmul(x, w, tm, tn, name):
    M, K = x.shape
    N = w.shape[1]
    return pl.pallas_call(
        _mm_kernel, grid=(M // tm, N // tn),
        in_specs=[pl.BlockSpec((tm, K), lambda i, j: (i, 0)), pl.BlockSpec((K, tn), lambda i, j: (0, j))],
        out_specs=pl.BlockSpec((tm, tn), lambda i, j: (i, j)),
        out_shape=jax.ShapeDtypeStruct((M, N), BF16),
        compiler_params=_params("arbitrary", "arbitrary"), name=name,
    )(x, w)


def _rope_kernel(pos_ref, freq_ref, cs_ref):
    ang = pos_ref[...].astype(F32) * freq_ref[...]
    keep = lax.broadcasted_iota(jnp.int32, ang.shape, 1) < LANE // 2
    cs_ref[:, :LANE] = jnp.where(keep, jnp.cos(ang), 0.0)
    cs_ref[:, LANE:] = jnp.where(keep, jnp.sin(ang), 0.0)


def _rope_tables(positions, rope_dim, tm=2048):
    T = positions.size
    half = rope_dim // 2
    freqs = ROPE_THETA ** (-jnp.arange(0, rope_dim, 2, dtype=F32) / rope_dim)
    freq_row = jnp.concatenate([freqs, freqs, jnp.zeros((LANE - 2 * half,), F32)]).reshape(1, LANE)
    tm = min(tm, T)
    return pl.pallas_call(
        _rope_kernel, grid=(T // tm,),
        in_specs=[pl.BlockSpec((tm, 1), lambda i: (i, 0)), pl.BlockSpec((1, LANE), lambda i: (0, 0))],
        out_specs=pl.BlockSpec((tm, 2 * LANE), lambda i: (i, 0)),
        out_shape=jax.ShapeDtypeStruct((T, 2 * LANE), F32),
        compiler_params=_params("arbitrary"), name="rope_tables",
    )(positions.reshape(T, 1), freq_row)


def _rope_lanes(v, cm, sm):
    return v * cm + pltpu.roll(v, LANE // 2, axis=1) * sm


def _q_up_kernel(x_ref, g_ref, cs_ref, w_ref, o_ref, xn_ref, *, scale):
    @pl.when(pl.program_id(1) == 0)
    def _():
        x = x_ref[...].astype(F32)
        r = lax.rsqrt(jnp.mean(x * x, -1, keepdims=True) + RMS_EPS)
        xn_ref[...] = (x * r * g_ref[...]).astype(BF16)

    y = jnp.dot(xn_ref[...], w_ref[...], preferred_element_type=F32)
    cm = cs_ref[:, :LANE] * scale
    sm = cs_ref[:, LANE:] * scale
    for h in range(y.shape[1] // (2 * LANE)):
        c0 = h * 2 * LANE
        o_ref[:, c0:c0 + LANE] = (y[:, c0:c0 + LANE] * scale).astype(BF16)
        o_ref[:, c0 + LANE:c0 + 2 * LANE] = _rope_lanes(y[:, c0 + LANE:c0 + 2 * LANE], cm, sm).astype(BF16)


def _q_up(proj, col_block, q_norm_g, cs, wq, scale, tm=512, tn=1024):
    T = proj.shape[0]
    R, N = wq.shape
    tn = min(tn, N)
    return pl.pallas_call(
        functools.partial(_q_up_kernel, scale=scale), grid=(T // tm, N // tn),
        in_specs=[pl.BlockSpec((tm, R), lambda i, j: (i, col_block)),
                  pl.BlockSpec((1, R), lambda i, j: (0, 0)),
                  pl.BlockSpec((tm, 2 * LANE), lambda i, j: (i, 0)),
                  pl.BlockSpec((R, tn), lambda i, j: (0, j))],
        out_specs=pl.BlockSpec((tm, tn), lambda i, j: (i, j)),
        out_shape=jax.ShapeDtypeStruct((T, N), BF16),
        scratch_shapes=[pltpu.VMEM((tm, R), BF16)],
        compiler_params=_params("arbitrary", "arbitrary"), name="q_up",
    )(proj, q_norm_g.reshape(1, R), cs, wq)


def _kv_up_kernel(x_ref, g_ref, kr_ref, cs_ref, w_ref, o_ref, kro_ref, xn_ref):
    @pl.when(pl.program_id(1) == 0)
    def _():
        x = x_ref[...].astype(F32)
        r = lax.rsqrt(jnp.mean(x * x, -1, keepdims=True) + RMS_EPS)
        xn_ref[...] = (x * r * g_ref[...]).astype(BF16)
        kro_ref[...] = _rope_lanes(kr_ref[...].astype(F32), cs_ref[:, :LANE], cs_ref[:, LANE:]).astype(BF16)

    o_ref[...] = jnp.dot(xn_ref[...], w_ref[...], preferred_element_type=F32).astype(BF16)


def _kv_up(proj, kv_block, kr_block, kv_norm_g, cs, wkv, tm=512, tn=1024):
    T = proj.shape[0]
    R, N = wkv.shape
    tn = min(tn, N)
    return pl.pallas_call(
        _kv_up_kernel, grid=(T // tm, N // tn),
        in_specs=[pl.BlockSpec((tm, R), lambda i, j: (i, kv_block)),
                  pl.BlockSpec((1, R), lambda i, j: (0, 0)),
                  pl.BlockSpec((tm, LANE), lambda i, j: (i, kr_block)),
                  pl.BlockSpec((tm, 2 * LANE), lambda i, j: (i, 0)),
                  pl.BlockSpec((R, tn), lambda i, j: (0, j))],
        out_specs=[pl.BlockSpec((tm, tn), lambda i, j: (i, j)),
                   pl.BlockSpec((tm, LANE), lambda i, j: (i, 0))],
        out_shape=[jax.ShapeDtypeStruct((T, N), BF16), jax.ShapeDtypeStruct((T, LANE), BF16)],
        scratch_shapes=[pltpu.VMEM((tm, R), BF16)],
        compiler_params=_params("arbitrary", "arbitrary"), name="kv_up",
    )(proj, kv_norm_g.reshape(1, R), proj, cs, wkv)


def _attn_kernel(q_ref, kn_ref, kr_ref, v_ref, o_ref, k_ref, *, tq):
    S = q_ref.shape[0]
    k_ref[:, :LANE] = kn_ref[...]
    k_ref[:, LANE:] = kr_ref[...]
    nt = (((1,), (1,)), ((), ()))
    row = lax.broadcasted_iota(jnp.int32, (tq, tq), 0)
    col = lax.broadcasted_iota(jnp.int32, (tq, tq), 1)
    tri = row >= col
    for qi in range(S // tq):
        lo = qi * tq
        q = q_ref[lo:lo + tq, :]
        s_d = lax.dot_general(q, k_ref[lo:lo + tq, :], nt, preferred_element_type=F32)
        s_d = jnp.where(tri, s_d, -jnp.inf)
        m = jnp.max(s_d, -1, keepdims=True)
        if qi > 0:
            s_o = lax.dot_general(q, k_ref[:lo, :], nt, preferred_element_type=F32)
            m = jnp.maximum(m, jnp.max(s_o, -1, keepdims=True))
        p_d = jnp.exp(s_d - m)
        l = jnp.sum(p_d, -1, keepdims=True)
        acc = jnp.dot(p_d.astype(BF16), v_ref[lo:lo + tq, :], preferred_element_type=F32)
        if qi > 0:
            p_o = jnp.exp(s_o - m)
            l = l + jnp.sum(p_o, -1, keepdims=True)
            acc = acc + jnp.dot(p_o.astype(BF16), v_ref[:lo, :], preferred_element_type=F32)
        o_ref[lo:lo + tq, :] = (acc / l).astype(BF16)


def _attention(q, kv, kr, B, S, H, tq=256):
    T = B * S
    tq = min(tq, S)
    return pl.pallas_call(
        functools.partial(_attn_kernel, tq=tq), grid=(B, H),
        in_specs=[pl.BlockSpec((S, 2 * LANE), lambda b, h: (b, h)),
                  pl.BlockSpec((S, LANE), lambda b, h: (b, h)),
                  pl.BlockSpec((S, LANE), lambda b, h: (b, 0)),
                  pl.BlockSpec((S, LANE), lambda b, h: (b, H + h))],
        out_specs=pl.BlockSpec((S, LANE), lambda b, h: (b, h)),
        out_shape=jax.ShapeDtypeStruct((T, H * LANE), BF16),
        scratch_shapes=[pltpu.VMEM((S, 2 * LANE), BF16)],
        compiler_params=_params("arbitrary", "arbitrary"), name="mla_attention",
    )(q, kv, kr, kv)


def _rglru_kernel(u_ref, g_ref, cw_ref, cb_ref, wa_ref, ba_ref, wx_ref, bx_ref, lam_ref, o_ref,
                  a_ref, b_ref):
    S, tc = u_ref.shape
    u = u_ref[...].astype(F32)
    row = lax.broadcasted_iota(jnp.int32, (S, tc), 0)
    cw = cw_ref[...]
    nconv = cw.shape[0]
    xc = u * cw[nconv - 1:nconv, :] + cb_ref[...]
    for d in range(1, nconv):
        xc = xc + jnp.where(row >= d, pltpu.roll(u, d, axis=0), 0.0) * cw[nconv - 1 - d:nconv - d, :]
    lam = lam_ref[...]
    z = -lam
    softplus = jnp.maximum(z, 0.0) + jnp.log1p(jnp.exp(-jnp.abs(z)))
    for blk in range(tc // LANE):
        sl = slice(blk * LANE, (blk + 1) * LANE)
        xb = xc[:, sl]
        xbb = xb.astype(BF16)
        r = _sigmoid(jnp.dot(xbb, wa_ref[blk], preferred_element_type=F32) + ba_ref[:, sl])
        gi = _sigmoid(jnp.dot(xbb, wx_ref[blk], preferred_element_type=F32) + bx_ref[:, sl])
        log_a = (-LRU_C) * r * softplus[:, sl]
        a_blk = jnp.exp(log_a)
        a_ref[:, sl] = a_blk
        b_ref[:, sl] = jnp.sqrt(-jnp.tanh(log_a) * (a_blk * a_blk + 1.0)) * (gi * xb)

    a = a_ref[...]
    b = b_ref[...]
    sub = row & 7
    for d in (1, 2, 4):
        ok = sub >= d
        b = b + a * jnp.where(ok, pltpu.roll(b, d, axis=0), 0.0)
        a = a * jnp.where(ok, pltpu.roll(a, d, axis=0), 1.0)
    a_ref[...] = a
    b_ref[...] = b

    def carry(t, hprev):
        r0 = pl.multiple_of(t * 8, 8)
        h = b_ref[pl.ds(r0, 8), :] + a_ref[pl.ds(r0, 8), :] * hprev
        b_ref[pl.ds(r0, 8), :] = h
        return jnp.broadcast_to(h[7:8, :], h.shape)

    lax.fori_loop(0, S // 8, carry, jnp.zeros((8, tc), F32))
    gg = g_ref[...].astype(F32)
    gelu = 0.5 * gg * (1.0 + jnp.tanh(math.sqrt(2.0 / math.pi) * (gg + 0.044715 * (gg * gg * gg))))
    o_ref[...] = (b_ref[...] * gelu).astype(BF16)


def _rglru(proj, B, S, C, conv_w, conv_b, wa, ba, wx, bx, lam, tc=256):
    T = B * S
    nb = tc // LANE
    ncb = C // tc
    vec = pl.BlockSpec((1, tc), lambda b, j: (0, j))
    wspec = pl.BlockSpec((nb, LANE, LANE), lambda b, j: (j, 0, 0))
    nconv = conv_w.shape[0]
    return pl.pallas_call(
        _rglru_kernel, grid=(B, ncb),
        in_specs=[pl.BlockSpec((S, tc), lambda b, j: (b, j)),
                  pl.BlockSpec((S, tc), lambda b, j: (b, ncb + j)),
                  pl.BlockSpec((nconv, tc), lambda b, j: (0, j)),
                  vec, wspec, vec, wspec, vec, vec],
        out_specs=pl.BlockSpec((S, tc), lambda b, j: (b, j)),
        out_shape=jax.ShapeDtypeStruct((T, C), BF16),
        scratch_shapes=[pltpu.VMEM((S, tc), F32), pltpu.VMEM((S, tc), F32)],
        compiler_params=_params("arbitrary", "arbitrary"), name="rglru",
    )(proj, proj, conv_w, conv_b.reshape(1, C), wa, ba.reshape(1, C), wx, bx.reshape(1, C),
      lam.reshape(1, C))


def _out_proj_kernel(a_ref, r_ref, ga_ref, gr_ref, hf_ref, w_ref, lg_ref, lb_ref, of_ref, ob_ref,
                     xs_ref, acc_ref, *, alpha):
    j = pl.program_id(1)
    nj = pl.num_programs(1)
    W = a_ref.shape[1]

    @pl.when(j == 0)
    def _():
        for src, g, c0 in ((a_ref, ga_ref, 0), (r_ref, gr_ref, W)):
            x = src[...].astype(F32)
            r = lax.rsqrt(jnp.mean(x * x, -1, keepdims=True) + RMS_EPS)
            xs_ref[:, c0:c0 + W] = (x * r * g[...]).astype(BF16)

    acc_ref[j] = jnp.dot(xs_ref[...], w_ref[...], preferred_element_type=F32)

    @pl.when(j == nj - 1)
    def _():
        tn = acc_ref.shape[2]
        for c in range(acc_ref.shape[0]):
            of_ref[:, c * tn:(c + 1) * tn] = alpha * hf_ref[:, c * tn:(c + 1) * tn] + acc_ref[c]
        y = _layer_norm(of_ref[...], lg_ref[...], lb_ref[...])
        of_ref[...] = y
        ob_ref[...] = y.astype(BF16)


def _out_proj(attn, rec, ga, gr, hf, w_o, lg, lb, alpha, tm=256, tn=512):
    T, W = attn.shape
    D = w_o.shape[1]
    tn = min(tn, D)
    row = pl.BlockSpec((tm, D), lambda i, j: (i, 0))
    half = pl.BlockSpec((tm, W), lambda i, j: (i, 0))
    vecw = pl.BlockSpec((1, W), lambda i, j: (0, 0))
    vecd = pl.BlockSpec((1, D), lambda i, j: (0, 0))
    return pl.pallas_call(
        functools.partial(_out_proj_kernel, alpha=alpha), grid=(T // tm, D // tn),
        in_specs=[half, half, vecw, vecw, row, pl.BlockSpec((2 * W, tn), lambda i, j: (0, j)), vecd, vecd],
        out_specs=[row, row],
        out_shape=[jax.ShapeDtypeStruct((T, D), F32), jax.ShapeDtypeStruct((T, D), BF16)],
        scratch_shapes=[pltpu.VMEM((tm, 2 * W), BF16), pltpu.VMEM((D // tn, tm, tn), F32)],
        compiler_params=_params("arbitrary", "arbitrary"), name="out_proj_ln",
    )(attn, rec, ga.reshape(1, W), gr.reshape(1, W), hf, w_o, lg.reshape(1, D), lb.reshape(1, D))


def _xattn_kernel(hb_ref, hf_ref, kv_ref, wq_ref, wo_ref, lg_ref, lb_ref, of_ref, op_ref, *, alpha, scale):
    XW = wq_ref.shape[1]
    D = hf_ref.shape[1]
    q = (jnp.dot(hb_ref[...], wq_ref[...], preferred_element_type=F32) * scale).astype(BF16)
    nt = (((1,), (1,)), ((), ()))
    outs = []
    for h in range(XW // X_HEAD_DIM):
        sl = slice(h * X_HEAD_DIM, (h + 1) * X_HEAD_DIM)
        s = lax.dot_general(q[:, sl], kv_ref[:, sl], nt, preferred_element_type=F32)
        m = jnp.max(s, -1, keepdims=True)
        p = jnp.exp(s - m)
        l = jnp.sum(p, -1, keepdims=True)
        o = jnp.dot(p.astype(BF16), kv_ref[:, XW + h * X_HEAD_DIM:XW + (h + 1) * X_HEAD_DIM],
                    preferred_element_type=F32)
        outs.append((o / l).astype(BF16))
    o = jnp.concatenate(outs, axis=-1)
    y = alpha * hf_ref[...] + jnp.dot(o, wo_ref[...], preferred_element_type=F32)
    y = _layer_norm(y, lg_ref[...], lb_ref[...])
    of_ref[...] = y
    op_ref[...] = _pack_pair(y[:, :D // 2], y[:, D // 2:])


def _xattn(hb, hf, kvm, wq, wo, lg, lb, alpha, S, M, tm=256):
    T, D = hf.shape
    XW = wq.shape[1]
    per_b = S // tm
    row = pl.BlockSpec((tm, D), lambda i: (i, 0))
    vecd = pl.BlockSpec((1, D), lambda i: (0, 0))
    return pl.pallas_call(
        functools.partial(_xattn_kernel, alpha=alpha, scale=X_HEAD_DIM ** -0.5), grid=(T // tm,),
        in_specs=[row, row, pl.BlockSpec((M, 2 * XW), lambda i: (i // per_b, 0)),
                  pl.BlockSpec((D, XW), lambda i: (0, 0)), pl.BlockSpec((XW, D), lambda i: (0, 0)),
                  vecd, vecd],
        out_specs=[row, pl.BlockSpec((tm, D // 2), lambda i: (i, 0))],
        out_shape=[jax.ShapeDtypeStruct((T, D), F32), jax.ShapeDtypeStruct((T, D // 2), U32)],
        compiler_params=_params("arbitrary"), name="mem_xattn_ln",
    )(hb, hf, kvm, wq, wo, lg.reshape(1, D), lb.reshape(1, D))


def _router_kernel(hf_ref, whi_ref, wlo_ref, bias_ref, idx_ref, gate_ref):
    x = hf_ref[...]
    xh = x.astype(BF16)
    xl = (x - xh.astype(F32)).astype(BF16)
    whi = whi_ref[...]
    logits = (jnp.dot(xh, whi, preferred_element_type=F32)
              + (jnp.dot(xl, whi, preferred_element_type=F32)
                 + jnp.dot(xh, wlo_ref[...], preferred_element_type=F32)))
    E = bias_ref.shape[0]
    tm = x.shape[0]
    per = E // N_GROUPS
    scores = _sigmoid(logits.T[:E, :])
    biased = scores + bias_ref[...]
    neg = -jnp.inf
    b3 = biased.reshape(N_GROUPS, per, tm)
    i3 = lax.broadcasted_iota(jnp.int32, b3.shape, 1)
    m1 = jnp.max(b3, axis=1, keepdims=True)
    first = jnp.min(jnp.where(b3 == m1, i3, per), axis=1, keepdims=True)
    m2 = jnp.max(jnp.where(i3 == first, neg, b3), axis=1, keepdims=True)
    gscore = (m1 + m2).reshape(N_GROUPS, tm)
    gi = lax.broadcasted_iota(jnp.int32, gscore.shape, 0)
    gsel = jnp.zeros(gscore.shape, F32)
    work = gscore
    for _ in range(TOPK_GROUPS):
        m = jnp.max(work, axis=0, keepdims=True)
        pick = jnp.min(jnp.where(work == m, gi, N_GROUPS), axis=0, keepdims=True)
        hit = gi == pick
        gsel = jnp.where(hit, 1.0, gsel)
        work = jnp.where(hit, neg, work)
    keep = jnp.broadcast_to(gsel.reshape(N_GROUPS, 1, tm), b3.shape).reshape(E, tm)
    masked = jnp.where(keep > 0.0, biased, neg)
    ei = lax.broadcasted_iota(jnp.int32, masked.shape, 0)
    ids, gates = [], []
    for _ in range(TOP_K):
        m = jnp.max(masked, axis=0, keepdims=True)
        pick = jnp.min(jnp.where(masked == m, ei, E), axis=0, keepdims=True)
        hit = ei == pick
        ids.append(pick)
        gates.append(jnp.sum(jnp.where(hit, scores, 0.0), axis=0, keepdims=True))
        masked = jnp.where(hit, neg, masked)
    g = jnp.concatenate(gates, axis=0)
    idx_ref[...] = jnp.concatenate(ids, axis=0)
    gate_ref[...] = g / jnp.sum(g, axis=0, keepdims=True) * ROUTED_SCALE


def _router(hf, w_router, bias, tm=256):
    T, D = hf.shape
    E = w_router.shape[1]
    wpad = jnp.pad(w_router, ((0, 0), (0, LANE - E)))
    whi = wpad.astype(BF16)
    wlo = (wpad - whi.astype(F32)).astype(BF16)
    wspec = pl.BlockSpec((D, LANE), lambda i: (0, 0))
    ospec = pl.BlockSpec((TOP_K, tm), lambda i: (0, i))
    return pl.pallas_call(
        _router_kernel, grid=(T // tm,),
        in_specs=[pl.BlockSpec((tm, D), lambda i: (i, 0)), wspec, wspec, pl.BlockSpec((E, 1), lambda i: (0, 0))],
        out_specs=[ospec, ospec],
        out_shape=[jax.ShapeDtypeStruct((TOP_K, T), jnp.int32), jax.ShapeDtypeStruct((TOP_K, T), F32)],
        compiler_params=_params("arbitrary"), name="router",
    )(hf, whi, wlo, bias.reshape(E, 1))


def _moe_kernel(be_ref, pos0_ref, lo_ref, nv_ref, sd_ref, x_hbm, w1_ref, w3_ref, w2_ref, y_hbm, xbuf, ybuf,
                gsem, ssem, *, n_tok, n_chunks):
    i = pl.program_id(0)
    nb = pl.num_programs(0)
    R, D2 = xbuf.shape[1], xbuf.shape[2]
    F = w2_ref.shape[0]
    cw = D2 // n_chunks
    per = R // n_chunks
    tok_of = (lambda d: d & (n_tok - 1)) if n_tok & (n_tok - 1) == 0 else (lambda d: lax.rem(d, n_tok))

    nxt = jnp.minimum(i + 1, nb - 1)
    prv = jnp.maximum(i - 1, 0)
    g_pos0 = pos0_ref[nxt]
    s_pos0 = pos0_ref[prv]
    s_lo = lo_ref[prv]
    s_nv = jnp.where(i > 0, nv_ref[prv], 0)

    def gather_rows(pos0, sl, r0, r1):
        for r in range(r0, r1):
            tok = tok_of(sd_ref[pos0 + r])
            pltpu.make_async_copy(x_hbm.at[pl.ds(tok, 1), :], xbuf.at[sl, pl.ds(r, 1), :], gsem.at[sl]).start()

    def gather_wait(sl):
        for r in range(R):
            pltpu.make_async_copy(x_hbm.at[pl.ds(0, 1), :], xbuf.at[sl, pl.ds(r, 1), :], gsem.at[sl]).wait()

    def scatter_rows(pos0, lo, nv, sl, r0, r1):
        spare = TOP_K * n_tok + sl * R
        for r in range(r0, r1):
            valid = (r - lo).astype(U32) < nv.astype(U32)
            dest = jnp.where(valid, sd_ref[pos0 + r], spare + r)
            pltpu.make_async_copy(ybuf.at[sl, pl.ds(r, 1), :], y_hbm.at[pl.ds(dest, 1), :], ssem.at[sl]).start()

    def scatter_wait(sl):
        for r in range(R):
            pltpu.make_async_copy(ybuf.at[sl, pl.ds(r, 1), :], y_hbm.at[pl.ds(0, 1), :], ssem.at[sl]).wait()

    def step(slot):
        other = 1 - slot
        if slot == 0:
            @pl.when(i == 0)
            def _():
                ybuf[...] = jnp.zeros(ybuf.shape, U32)
                gather_rows(pos0_ref[0], 0, 0, R)

        gather_wait(slot)
        acc = None
        for c in range(n_chunks):
            gather_rows(g_pos0, other, c * per, (c + 1) * per)
            lo, hi = _unpack_pair(xbuf[slot, :, c * cw:(c + 1) * cw])
            xc = jnp.concatenate([lo.astype(BF16), hi.astype(BF16)], axis=1)
            ra = slice(c * cw, (c + 1) * cw)
            rb = slice(D2 + c * cw, D2 + (c + 1) * cw)
            wc = jnp.concatenate([jnp.concatenate([w1_ref[ra, :], w3_ref[ra, :]], axis=1),
                                  jnp.concatenate([w1_ref[rb, :], w3_ref[rb, :]], axis=1)], axis=0).astype(BF16)
            part = jnp.dot(xc, wc, preferred_element_type=F32)
            acc = part if acc is None else acc + part
        h1 = acc[:, :F]
        hid = (h1 * _sigmoid(h1) * acc[:, F:]).astype(BF16)

        @pl.when(i >= 1)
        def _():
            scatter_wait(slot)

        for c in range(n_chunks):
            scatter_rows(s_pos0, s_lo, s_nv, other, c * per, (c + 1) * per)
            wc = jnp.concatenate([w2_ref[:, c * cw:(c + 1) * cw], w2_ref[:, D2 + c * cw:D2 + (c + 1) * cw]],
                                 axis=1).astype(BF16)
            yc = jnp.dot(hid, wc, preferred_element_type=F32)
            ybuf[slot, :, c * cw:(c + 1) * cw] = _pack_pair(yc[:, :cw], yc[:, cw:])

        @pl.when(i == nb - 1)
        def _():
            scatter_rows(pos0_ref[i], lo_ref[i], nv_ref[i], slot, 0, R)
            gather_wait(other)
            scatter_wait(other)
            scatter_wait(slot)

    parity = lax.rem(i, 2)
    for s in (0, 1):
        pl.when(parity == s)(functools.partial(step, s))


def _moe(hp, plan, w_e1, w_e3, w_e2, layer, n_blocks):
    T, D2 = hp.shape
    _, E, D, F = w_e1.shape
    n_chunks = max(1, min(8, D2 // (2 * LANE)))
    wmap = lambda i, be, p0, lo, nv, sd: (layer, be[i], 0, 0)
    grid_spec = pltpu.PrefetchScalarGridSpec(
        num_scalar_prefetch=5, grid=(n_blocks,),
        in_specs=[pl.BlockSpec(memory_space=pl.ANY),
                  pl.BlockSpec((None, None, D, F), wmap),
                  pl.BlockSpec((None, None, D, F), wmap),
                  pl.BlockSpec((None, None, F, D), wmap)],
        out_specs=pl.BlockSpec(memory_space=pl.ANY),
        scratch_shapes=[pltpu.VMEM((2, MOE_ROWS, D2), U32), pltpu.VMEM((2, MOE_ROWS, D2), U32),
                        pltpu.SemaphoreType.DMA((2,)), pltpu.SemaphoreType.DMA((2,))])
    return pl.pallas_call(
        functools.partial(_moe_kernel, n_tok=T, n_chunks=n_chunks), grid_spec=grid_spec,
        out_shape=jax.ShapeDtypeStruct((TOP_K * T + 2 * MOE_ROWS, D2), U32),
        compiler_params=_params("arbitrary"), name="moe_experts",
    )(*plan, hp, w_e1, w_e3, w_e2)


def _route_plan(idx_t, n_experts, n_blocks):
    K, T = idx_t.shape
    e_flat = idx_t.T.reshape(T * K)
    sorted_e, sorted_m = lax.sort((e_flat, jnp.arange(T * K, dtype=jnp.int32)), num_keys=1, is_stable=True)
    edges = jnp.arange(n_experts + 1, dtype=jnp.int32)
    start = jnp.sum((sorted_e[None, :] < edges[:, None]).astype(jnp.int32), axis=1)
    counts = start[1:] - start[:-1]
    nblk = (counts + MOE_ROWS - 1) // MOE_ROWS
    bend = jnp.cumsum(nblk)
    j = jnp.arange(n_blocks, dtype=jnp.int32)
    used = j < bend[-1]
    be = jnp.minimum(jnp.sum((bend[None, :] <= j[:, None]).astype(jnp.int32), axis=1), n_experts - 1)
    off = (j - (bend - nblk)[be]) * MOE_ROWS
    nv = jnp.where(used, jnp.clip(counts[be] - off, 0, MOE_ROWS), 0).astype(jnp.int32)
    first = jnp.where(used, start[be] + off, 0).astype(jnp.int32)
    pos0 = jnp.minimum(first, T * K - MOE_ROWS)
    lo = first - pos0
    sorted_dest = (sorted_m % K) * T + sorted_m // K
    return be, pos0, lo, nv, sorted_dest


def _combine_kernel(*refs, alpha):
    hf_ref, hp_ref, gate_ref = refs[:3]
    y_refs = refs[3:3 + TOP_K]
    w13_ref, w2_ref, lg_ref, lb_ref, of_ref, ob_ref = refs[3 + TOP_K:]
    D2 = hp_ref.shape[1]
    F = w2_ref.shape[0]
    xlo, xhi = _unpack_pair(hp_ref[...])
    h13 = (jnp.dot(xlo.astype(BF16), w13_ref[:D2, :], preferred_element_type=F32)
           + jnp.dot(xhi.astype(BF16), w13_ref[D2:, :], preferred_element_type=F32))
    h1 = h13[:, :F]
    hid = (h1 * _sigmoid(h1) * h13[:, F:]).astype(BF16)
    sh = jnp.dot(hid, w2_ref[...], preferred_element_type=F32)
    acc_lo = sh[:, :D2]
    acc_hi = sh[:, D2:]
    gate = gate_ref[...]
    for k in range(TOP_K):
        ylo, yhi = _unpack_pair(y_refs[k][...])
        gk = gate[:, k:k + 1]
        acc_lo = acc_lo + ylo * gk
        acc_hi = acc_hi + yhi * gk
    of_ref[:, :D2] = alpha * hf_ref[:, :D2] + acc_lo
    of_ref[:, D2:] = alpha * hf_ref[:, D2:] + acc_hi
    y = _layer_norm(of_ref[...], lg_ref[...], lb_ref[...])
    of_ref[...] = y
    ob_ref[...] = y.astype(BF16)


def _combine(hf, hp, gate, yk, w13, w2, lg, lb, alpha, tm=128):
    T, D = hf.shape
    D2 = D // 2
    F2 = w13.shape[1]
    nt = T // tm
    row = pl.BlockSpec((tm, D), lambda i: (i, 0))
    vecd = pl.BlockSpec((1, D), lambda i: (0, 0))
    y_specs = [pl.BlockSpec((tm, D2), functools.partial(lambda i, k: (k * nt + i, 0), k=k)) for k in range(TOP_K)]
    return pl.pallas_call(
        functools.partial(_combine_kernel, alpha=alpha), grid=(nt,),
        in_specs=[row, pl.BlockSpec((tm, D2), lambda i: (i, 0)), pl.BlockSpec((tm, TOP_K), lambda i: (i, 0))]
        + y_specs + [pl.BlockSpec((D, F2), lambda i: (0, 0)), pl.BlockSpec((F2 // 2, D), lambda i: (0, 0)), vecd, vecd],
        out_specs=[row, row],
        out_shape=[jax.ShapeDtypeStruct((T, D), F32), jax.ShapeDtypeStruct((T, D), BF16)],
        compiler_params=_params("arbitrary"), name="moe_combine_ln",
    )(hf, hp, gate, *([yk] * TOP_K), w13, w2, lg.reshape(1, D), lb.reshape(1, D))


def _rot_half_cols(w):
    half = w.shape[-1] // 2
    return jnp.concatenate([-w[..., half:], w[..., :half]], axis=-1)


def _pad_cols(w, n):
    return jnp.pad(w, ((0, 0), (0, n - w.shape[1])))


def kernel(x, mem, positions, ln_in_g, ln_in_b, w_in, q_norm_g, kv_norm_g, w_q_b, w_kv_b, conv_w, conv_b, w_rg_a, b_rg_a, w_rg_x, b_rg_x, lru_lambda, mla_out_g, rnn_out_g, w_o, ln1_g, ln1_b, w_xq, w_xk, w_xv, w_xo, ln2_g, ln2_b, w_router, router_bias, w_e1, w_e3, w_e2, w_s1, w_s3, w_s2, ln3_g, ln3_b):
    B, S, D = x.shape
    M = mem.shape[1]
    L = w_in.shape[0]
    T = B * S
    q_lora = q_norm_g.shape[1]
    kv_lora = kv_norm_g.shape[1]
    H, qk_dim = w_q_b.shape[2], w_q_b.shape[3]
    C = conv_b.shape[1]
    rope = w_in.shape[2] - q_lora - kv_lora - 2 * C
    nope = qk_dim - rope
    E = w_router.shape[2]
    assert nope == LANE and rope == LANE // 2 and w_kv_b.shape[3] - nope == LANE
    assert T * TOP_K >= MOE_ROWS
    alpha = (2.0 * L) ** 0.25
    n_blocks = -(-(T * TOP_K) // MOE_ROWS) + E

    off_kv, off_kr = q_lora, q_lora + kv_lora
    off_u, off_g = off_kr + rope, off_kr + rope + C
    col_kv, col_q, col_kr = 2 * C, 2 * C + kv_lora, 2 * C + kv_lora + q_lora
    assert col_kv % kv_lora == 0 and col_q % q_lora == 0 and col_kr % LANE == 0
    n_proj = -(-(col_kr + LANE) // 512) * 512

    cs = _rope_tables(positions, rope)
    hf, hb = _ln_in(x.reshape(T, D), ln_in_g, ln_in_b)
    memf = mem.reshape(B * M, D)

    for l in range(L):
        wi = w_in[l]
        w_kr = wi[:, off_kr:off_kr + rope]
        w_proj = _pad_cols(jnp.concatenate(
            [wi[:, off_u:off_u + C], wi[:, off_g:off_g + C], wi[:, off_kv:off_kv + kv_lora], wi[:, :q_lora],
             w_kr, _rot_half_cols(w_kr)], axis=1), n_proj).astype(BF16)
        wq3 = w_q_b[l]
        wq = jnp.concatenate([wq3[..., :nope], wq3[..., nope:], _rot_half_cols(wq3[..., nope:])],
                             axis=-1).reshape(q_lora, H * 2 * LANE).astype(BF16)
        wkv3 = w_kv_b[l]
        wkv = jnp.concatenate([wkv3[..., :nope].reshape(kv_lora, H * LANE),
                               wkv3[..., nope:].reshape(kv_lora, H * LANE)], axis=1).astype(BF16)

        proj = _matmul(hb, w_proj, tm=min(1024, T), tn=512, name="in_proj")
        q = _q_up(proj, col_q // q_lora, q_norm_g[l], cs, wq, qk_dim ** -0.5, tm=min(512, T))
        kv, kr = _kv_up(proj, col_kv // kv_lora, col_kr // LANE, kv_norm_g[l], cs, wkv, tm=min(512, T))
        attn = _attention(q, kv, kr, B, S, H)
        rec = _rglru(proj, B, S, C, conv_w[l].reshape(conv_w.shape[1], C), conv_b[l],
                     w_rg_a[l].astype(BF16), b_rg_a[l], w_rg_x[l].astype(BF16), b_rg_x[l], lru_lambda[l])
        hf, hb = _out_proj(attn, rec, mla_out_g[l], rnn_out_g[l], hf, w_o[l].astype(BF16), ln1_g[l], ln1_b[l], alpha)

        w_mkv = jnp.concatenate([w_xk[l], w_xv[l]], axis=1).astype(BF16)
        kvm = _matmul(memf, w_mkv, tm=min(256, B * M), tn=w_mkv.shape[1], name="mem_kv")
        hf, hp = _xattn(hb, hf, kvm, w_xq[l].astype(BF16), w_xo[l].astype(BF16), ln2_g[l], ln2_b[l], alpha, S, M)

        idx_t, gate_t = _router(hf, w_router[l], router_bias[l])
        yk = _moe(hp, _route_plan(idx_t, E, n_blocks), w_e1, w_e3, w_e2, l, n_blocks)
        ws13 = jnp.concatenate([w_s1[l], w_s3[l]], axis=-1).astype(BF16)
        hf, hb = _combine(hf, hp, gate_t.T, yk, ws13, w_s2[l].astype(BF16), ln3_g[l], ln3_b[l], alpha)

    return hf.reshape(B, S, D)
```

```python
import functools
import math

import jax
import jax.numpy as jnp
from jax import lax
from jax.experimental import pallas as pl
from jax.experimental.pallas import tpu as pltpu

F32 = jnp.float32
BF16 = jnp.bfloat16
U32 = jnp.uint32

LANE = 128
VMEM_LIMIT = 56 * 1024 * 1024
OUT_PROJ_VMEM_LIMIT = 60 * 1024 * 1024

ROPE_THETA = 10000.0
X_HEAD_DIM = 128
TOP_K = 8
N_GROUPS = 8
TOPK_GROUPS = 4
ROUTED_SCALE = 2.5
LRU_C = 8.0
LN_EPS = 1e-5
RMS_EPS = 1e-6
MOE_ROWS = 256


def _params(*sem):
    return pltpu.CompilerParams(dimension_semantics=sem, vmem_limit_bytes=VMEM_LIMIT)


def _layer_norm(y, g, b):
    mu = jnp.mean(y, -1, keepdims=True)
    yc = y - mu
    var = jnp.mean(yc * yc, -1, keepdims=True)
    return yc * lax.rsqrt(var + LN_EPS) * g + b


def _sigmoid(x):
    return 1.0 / (1.0 + jnp.exp(-x))


def _bits(x):
    return lax.bitcast_convert_type(x, U32)


def _pack_pair(lo, hi):
    lo_b = _bits(lo.astype(BF16).astype(F32))
    hi_b = _bits(hi.astype(BF16).astype(F32))
    return hi_b | lax.shift_right_logical(lo_b, jnp.uint32(16))


def _unpack_pair(w):
    lo = lax.bitcast_convert_type(lax.shift_left(w, jnp.uint32(16)), F32)
    hi = lax.bitcast_convert_type(w & jnp.uint32(0xFFFF0000), F32)
    return lo, hi


def _ln_in_kernel(x_ref, g_ref, b_ref, hf_ref, hb_ref):
    y = _layer_norm(x_ref[...], g_ref[...], b_ref[...])
    hf_ref[...] = y
    hb_ref[...] = y.astype(BF16)


def _ln_in(x, g, b, tm=256):
    T, D = x.shape
    row = pl.BlockSpec((tm, D), lambda i: (i, 0))
    vec = pl.BlockSpec((1, D), lambda i: (0, 0))
    return pl.pallas_call(
        _ln_in_kernel, grid=(T // tm,),
        in_specs=[row, vec, vec], out_specs=[row, row],
        out_shape=[jax.ShapeDtypeStruct((T, D), F32), jax.ShapeDtypeStruct((T, D), BF16)],
        compiler_params=_params("arbitrary"), name="ln_in",
    )(x, g.reshape(1, D), b.reshape(1, D))


def _mm_kernel(x_ref, w_ref, o_ref):
    o_ref[...] = jnp.dot(x_ref[...].astype(BF16), w_ref[...],
                         preferred_element_type=F32).astype(o_ref.dtype)


def _matmul(x, w, tm, tn, name):
    M, K = x.shape
    N = w.shape[1]
    return pl.pallas_call(
        _mm_kernel, grid=(M // tm, N // tn),
        in_specs=[pl.BlockSpec((tm, K), lambda i, j: (i, 0)), pl.BlockSpec((K, tn), lambda i, j: (0, j))],
        out_specs=pl.BlockSpec((tm, tn), lambda i, j: (i, j)),
        out_shape=jax.ShapeDtypeStruct((M, N), BF16),
        compiler_params=_params("arbitrary", "arbitrary"), name=name,
    )(x, w)


def _rope_kernel(pos_ref, freq_ref, cs_ref):
    ang = pos_ref[...].astype(F32) * freq_ref[...]
    keep = lax.broadcasted_iota(jnp.int32, ang.shape, 1) < LANE // 2
    cs_ref[:, :LANE] = jnp.where(keep, jnp.cos(ang), 0.0)
    cs_ref[:, LANE:] = jnp.where(keep, jnp.sin(ang), 0.0)


def _rope_tables(positions, rope_dim, tm=2048):
    T = positions.size
    half = rope_dim // 2
    freqs = ROPE_THETA ** (-jnp.arange(0, rope_dim, 2, dtype=F32) / rope_dim)
    freq_row = jnp.concatenate([freqs, freqs, jnp.zeros((LANE - 2 * half,), F32)]).reshape(1, LANE)
    tm = min(tm, T)
    return pl.pallas_call(
        _rope_kernel, grid=(T // tm,),
        in_specs=[pl.BlockSpec((tm, 1), lambda i: (i, 0)), pl.BlockSpec((1, LANE), lambda i: (0, 0))],
        out_specs=pl.BlockSpec((tm, 2 * LANE), lambda i: (i, 0)),
        out_shape=jax.ShapeDtypeStruct((T, 2 * LANE), F32),
        compiler_params=_params("arbitrary"), name="rope_tables",
    )(positions.reshape(T, 1), freq_row)


def _rope_lanes(v, cm, sm):
    return v * cm + pltpu.roll(v, LANE // 2, axis=1) * sm


def _q_up_kernel(x_ref, g_ref, cs_ref, w_ref, o_ref, xn_ref, *, scale):
    @pl.when(pl.program_id(1) == 0)
    def _():
        x = x_ref[...].astype(F32)
        r = lax.rsqrt(jnp.mean(x * x, -1, keepdims=True) + RMS_EPS)
        xn_ref[...] = (x * r * g_ref[...]).astype(BF16)

    y = jnp.dot(xn_ref[...], w_ref[...], preferred_element_type=F32)
    cm = cs_ref[:, :LANE] * scale
    sm = cs_ref[:, LANE:] * scale
    for h in range(y.shape[1] // (2 * LANE)):
        c0 = h * 2 * LANE
        o_ref[:, c0:c0 + LANE] = (y[:, c0:c0 + LANE] * scale).astype(BF16)
        o_ref[:, c0 + LANE:c0 + 2 * LANE] = _rope_lanes(y[:, c0 + LANE:c0 + 2 * LANE], cm, sm).astype(BF16)


def _q_up(proj, col_block, q_norm_g, cs, wq, scale, tm=512, tn=1024):
    T = proj.shape[0]
    R, N = wq.shape
    tn = min(tn, N)
    return pl.pallas_call(
        functools.partial(_q_up_kernel, scale=scale), grid=(T // tm, N // tn),
        in_specs=[pl.BlockSpec((tm, R), lambda i, j: (i, col_block)),
                  pl.BlockSpec((1, R), lambda i, j: (0, 0)),
                  pl.BlockSpec((tm, 2 * LANE), lambda i, j: (i, 0)),
                  pl.BlockSpec((R, tn), lambda i, j: (0, j))],
        out_specs=pl.BlockSpec((tm, tn), lambda i, j: (i, j)),
        out_shape=jax.ShapeDtypeStruct((T, N), BF16),
        scratch_shapes=[pltpu.VMEM((tm, R), BF16)],
        compiler_params=_params("arbitrary", "arbitrary"), name="q_up",
    )(proj, q_norm_g.reshape(1, R), cs, wq)


def _kv_up_kernel(x_ref, g_ref, kr_ref, cs_ref, w_ref, o_ref, kro_ref, xn_ref):
    @pl.when(pl.program_id(1) == 0)
    def _():
        x = x_ref[...].astype(F32)
        r = lax.rsqrt(jnp.mean(x * x, -1, keepdims=True) + RMS_EPS)
        xn_ref[...] = (x * r * g_ref[...]).astype(BF16)
        kro_ref[...] = _rope_lanes(kr_ref[...].astype(F32), cs_ref[:, :LANE], cs_ref[:, LANE:]).astype(BF16)

    o_ref[...] = jnp.dot(xn_ref[...], w_ref[...], preferred_element_type=F32).astype(BF16)


def _kv_up(proj, kv_block, kr_block, kv_norm_g, cs, wkv, tm=512, tn=1024):
    T = proj.shape[0]
    R, N = wkv.shape
    tn = min(tn, N)
    return pl.pallas_call(
        _kv_up_kernel, grid=(T // tm, N // tn),
        in_specs=[pl.BlockSpec((tm, R), lambda i, j: (i, kv_block)),
                  pl.BlockSpec((1, R), lambda i, j: (0, 0)),
                  pl.BlockSpec((tm, LANE), lambda i, j: (i, kr_block)),
                  pl.BlockSpec((tm, 2 * LANE), lambda i, j: (i, 0)),
                  pl.BlockSpec((R, tn), lambda i, j: (0, j))],
        out_specs=[pl.BlockSpec((tm, tn), lambda i, j: (i, j)),
                   pl.BlockSpec((tm, LANE), lambda i, j: (i, 0))],
        out_shape=[jax.ShapeDtypeStruct((T, N), BF16), jax.ShapeDtypeStruct((T, LANE), BF16)],
        scratch_shapes=[pltpu.VMEM((tm, R), BF16)],
        compiler_params=_params("arbitrary", "arbitrary"), name="kv_up",
    )(proj, kv_norm_g.reshape(1, R), proj, cs, wkv)


def _attn_kernel(q_ref, kn_ref, kr_ref, v_ref, o_ref, k_ref, *, tq):
    S = q_ref.shape[0]
    k_ref[:, :LANE] = kn_ref[...]
    k_ref[:, LANE:] = kr_ref[...]
    nt = (((1,), (1,)), ((), ()))
    row = lax.broadcasted_iota(jnp.int32, (tq, tq), 0)
    col = lax.broadcasted_iota(jnp.int32, (tq, tq), 1)
    tri = row >= col
    for qi in range(S // tq):
        lo = qi * tq
        q = q_ref[lo:lo + tq, :]
        s_d = lax.dot_general(q, k_ref[lo:lo + tq, :], nt, preferred_element_type=F32)
        s_d = jnp.where(tri, s_d, -jnp.inf)
        m = jnp.max(s_d, -1, keepdims=True)
        if qi > 0:
            s_o = lax.dot_general(q, k_ref[:lo, :], nt, preferred_element_type=F32)
            m = jnp.maximum(m, jnp.max(s_o, -1, keepdims=True))
        p_d = jnp.exp(s_d - m)
        l = jnp.sum(p_d, -1, keepdims=True)
        acc = jnp.dot(p_d.astype(BF16), v_ref[lo:lo + tq, :], preferred_element_type=F32)
        if qi > 0:
            p_o = jnp.exp(s_o - m)
            l = l + jnp.sum(p_o, -1, keepdims=True)
            acc = acc + jnp.dot(p_o.astype(BF16), v_ref[:lo, :], preferred_element_type=F32)
        o_ref[lo:lo + tq, :] = (acc / l).astype(BF16)


def _attention(q, kv, kr, B, S, H, tq=256):
    T = B * S
    tq = min(tq, S)
    return pl.pallas_call(
        functools.partial(_attn_kernel, tq=tq), grid=(B, H),
        in_specs=[pl.BlockSpec((S, 2 * LANE), lambda b, h: (b, h)),
                  pl.BlockSpec((S, LANE), lambda b, h: (b, h)),
                  pl.BlockSpec((S, LANE), lambda b, h: (b, 0)),
                  pl.BlockSpec((S, LANE), lambda b, h: (b, H + h))],
        out_specs=pl.BlockSpec((S, LANE), lambda b, h: (b, h)),
        out_shape=jax.ShapeDtypeStruct((T, H * LANE), BF16),
        scratch_shapes=[pltpu.VMEM((S, 2 * LANE), BF16)],
        compiler_params=_params("arbitrary", "arbitrary"), name="mla_attention",
    )(q, kv, kr, kv)


def _rglru_kernel(u_ref, g_ref, cw_ref, cb_ref, wa_ref, ba_ref, wx_ref, bx_ref, lam_ref, o_ref,
                  a_ref, b_ref):
    S, tc = u_ref.shape
    u = u_ref[...].astype(F32)
    row = lax.broadcasted_iota(jnp.int32, (S, tc), 0)
    cw = cw_ref[...]
    nconv = cw.shape[0]
    xc = u * cw[nconv - 1:nconv, :] + cb_ref[...]
    for d in range(1, nconv):
        xc = xc + jnp.where(row >= d, pltpu.roll(u, d, axis=0), 0.0) * cw[nconv - 1 - d:nconv - d, :]
    lam = lam_ref[...]
    z = -lam
    softplus = jnp.maximum(z, 0.0) + jnp.log1p(jnp.exp(-jnp.abs(z)))
    for blk in range(tc // LANE):
        sl = slice(blk * LANE, (blk + 1) * LANE)
        xb = xc[:, sl]
        xbb = xb.astype(BF16)
        r = _sigmoid(jnp.dot(xbb, wa_ref[blk], preferred_element_type=F32) + ba_ref[:, sl])
        gi = _sigmoid(jnp.dot(xbb, wx_ref[blk], preferred_element_type=F32) + bx_ref[:, sl])
        log_a = (-LRU_C) * r * softplus[:, sl]
        a_blk = jnp.exp(log_a)
        a_ref[:, sl] = a_blk
        b_ref[:, sl] = jnp.sqrt(-jnp.tanh(log_a) * (a_blk * a_blk + 1.0)) * (gi * xb)

    a = a_ref[...]
    b = b_ref[...]
    sub = row & 7
    for d in (1, 2, 4):
        ok = sub >= d
        b = b + a * jnp.where(ok, pltpu.roll(b, d, axis=0), 0.0)
        a = a * jnp.where(ok, pltpu.roll(a, d, axis=0), 1.0)
    a_ref[...] = a
    b_ref[...] = b

    def carry(t, hprev):
        r0 = pl.multiple_of(t * 8, 8)
        h = b_ref[pl.ds(r0, 8), :] + a_ref[pl.ds(r0, 8), :] * hprev
        b_ref[pl.ds(r0, 8), :] = h
        return jnp.broadcast_to(h[7:8, :], h.shape)

    lax.fori_loop(0, S // 8, carry, jnp.zeros((8, tc), F32))
    gg = g_ref[...].astype(F32)
    gelu = 0.5 * gg * (1.0 + jnp.tanh(math.sqrt(2.0 / math.pi) * (gg + 0.044715 * (gg * gg * gg))))
    o_ref[...] = (b_ref[...] * gelu).astype(BF16)


def _rglru(proj, B, S, C, conv_w, conv_b, wa, ba, wx, bx, lam, tc=256):
    T = B * S
    nb = tc // LANE
    ncb = C // tc
    vec = pl.BlockSpec((1, tc), lambda b, j: (0, j))
    wspec = pl.BlockSpec((nb, LANE, LANE), lambda b, j: (j, 0, 0))
    nconv = conv_w.shape[0]
    return pl.pallas_call(
        _rglru_kernel, grid=(B, ncb),
        in_specs=[pl.BlockSpec((S, tc), lambda b, j: (b, j)),
                  pl.BlockSpec((S, tc), lambda b, j: (b, ncb + j)),
                  pl.BlockSpec((nconv, tc), lambda b, j: (0, j)),
                  vec, wspec, vec, wspec, vec, vec],
        out_specs=pl.BlockSpec((S, tc), lambda b, j: (b, j)),
        out_shape=jax.ShapeDtypeStruct((T, C), BF16),
        scratch_shapes=[pltpu.VMEM((S, tc), F32), pltpu.VMEM((S, tc), F32)],
        compiler_params=_params("arbitrary", "arbitrary"), name="rglru",
    )(proj, proj, conv_w, conv_b.reshape(1, C), wa, ba.reshape(1, C), wx, bx.reshape(1, C),
      lam.reshape(1, C))


def _out_proj_kernel(a_ref, r_ref, ga_ref, gr_ref, hf_ref, w_ref, lg_ref, lb_ref, of_ref, ob_ref,
                     xs_ref, acc_ref, *, alpha):
    j = pl.program_id(1)
    nj = pl.num_programs(1)
    W = a_ref.shape[1]

    @pl.when(j == 0)
    def _():
        for src, g, c0 in ((a_ref, ga_ref, 0), (r_ref, gr_ref, W)):
            x = src[...].astype(F32)
            r = lax.rsqrt(jnp.mean(x * x, -1, keepdims=True) + RMS_EPS)
            xs_ref[:, c0:c0 + W] = (x * r * g[...]).astype(BF16)

    acc_ref[j] = jnp.dot(xs_ref[...], w_ref[...], preferred_element_type=F32)

    @pl.when(j == nj - 1)
    def _():
        nc, tm, tn = acc_ref.shape
        rc = min(tm, LANE)
        for r0 in range(0, tm, rc):
            rows = slice(r0, r0 + rc)
            y = jnp.concatenate([alpha * hf_ref[rows, c * tn:(c + 1) * tn] + acc_ref[c, rows, :]
                                 for c in range(nc)], axis=1)
            y = _layer_norm(y, lg_ref[...], lb_ref[...])
            of_ref[rows, :] = y
            ob_ref[rows, :] = y.astype(BF16)


def _out_proj(attn, rec, ga, gr, hf, w_o, lg, lb, alpha, tm=512, tn=512):
    T, W = attn.shape
    D = w_o.shape[1]
    tn = min(tn, D)
    row = pl.BlockSpec((tm, D), lambda i, j: (i, 0), pipeline_mode=pl.Buffered(1))
    half = pl.BlockSpec((tm, W), lambda i, j: (i, 0))
    vecw = pl.BlockSpec((1, W), lambda i, j: (0, 0))
    vecd = pl.BlockSpec((1, D), lambda i, j: (0, 0))
    return pl.pallas_call(
        functools.partial(_out_proj_kernel, alpha=alpha), grid=(T // tm, D // tn),
        in_specs=[half, half, vecw, vecw, row, pl.BlockSpec((2 * W, tn), lambda i, j: (0, j)), vecd, vecd],
        out_specs=[row, row],
        out_shape=[jax.ShapeDtypeStruct((T, D), F32), jax.ShapeDtypeStruct((T, D), BF16)],
        scratch_shapes=[pltpu.VMEM((tm, 2 * W), BF16), pltpu.VMEM((D // tn, tm, tn), F32)],
        compiler_params=pltpu.CompilerParams(dimension_semantics=("arbitrary", "arbitrary"),
                                             vmem_limit_bytes=OUT_PROJ_VMEM_LIMIT), name="out_proj_ln",
    )(attn, rec, ga.reshape(1, W), gr.reshape(1, W), hf, w_o, lg.reshape(1, D), lb.reshape(1, D))


def _xattn_kernel(hb_ref, hf_ref, kv_ref, wq_ref, wo_ref, lg_ref, lb_ref, of_ref, op_ref, *, alpha, scale):
    XW = wq_ref.shape[1]
    D = hf_ref.shape[1]
    q = (jnp.dot(hb_ref[...], wq_ref[...], preferred_element_type=F32) * scale).astype(BF16)
    nt = (((1,), (1,)), ((), ()))
    outs = []
    for h in range(XW // X_HEAD_DIM):
        sl = slice(h * X_HEAD_DIM, (h + 1) * X_HEAD_DIM)
        s = lax.dot_general(q[:, sl], kv_ref[:, sl], nt, preferred_element_type=F32)
        m = jnp.max(s, -1, keepdims=True)
        p = jnp.exp(s - m)
        l = jnp.sum(p, -1, keepdims=True)
        o = jnp.dot(p.astype(BF16), kv_ref[:, XW + h * X_HEAD_DIM:XW + (h + 1) * X_HEAD_DIM],
                    preferred_element_type=F32)
        outs.append((o / l).astype(BF16))
    o = jnp.concatenate(outs, axis=-1)
    y = alpha * hf_ref[...] + jnp.dot(o, wo_ref[...], preferred_element_type=F32)
    y = _layer_norm(y, lg_ref[...], lb_ref[...])
    of_ref[...] = y
    op_ref[...] = _pack_pair(y[:, :D // 2], y[:, D // 2:])


def _xattn(hb, hf, kvm, wq, wo, lg, lb, alpha, S, M, tm=256):
    T, D = hf.shape
    XW = wq.shape[1]
    per_b = S // tm
    row = pl.BlockSpec((tm, D), lambda i: (i, 0))
    vecd = pl.BlockSpec((1, D), lambda i: (0, 0))
    return pl.pallas_call(
        functools.partial(_xattn_kernel, alpha=alpha, scale=X_HEAD_DIM ** -0.5), grid=(T // tm,),
        in_specs=[row, row, pl.BlockSpec((M, 2 * XW), lambda i: (i // per_b, 0)),
                  pl.BlockSpec((D, XW), lambda i: (0, 0)), pl.BlockSpec((XW, D), lambda i: (0, 0)),
                  vecd, vecd],
        out_specs=[row, pl.BlockSpec((tm, D // 2), lambda i: (i, 0))],
        out_shape=[jax.ShapeDtypeStruct((T, D), F32), jax.ShapeDtypeStruct((T, D // 2), U32)],
        compiler_params=_params("arbitrary"), name="mem_xattn_ln",
    )(hb, hf, kvm, wq, wo, lg.reshape(1, D), lb.reshape(1, D))


def _router_kernel(hf_ref, whi_ref, wlo_ref, bias_ref, idx_ref, gate_ref):
    x = hf_ref[...]
    xh = x.astype(BF16)
    xl = (x - xh.astype(F32)).astype(BF16)
    whi = whi_ref[...]
    logits = (jnp.dot(xh, whi, preferred_element_type=F32)
              + (jnp.dot(xl, whi, preferred_element_type=F32)
                 + jnp.dot(xh, wlo_ref[...], preferred_element_type=F32)))
    E = bias_ref.shape[0]
    tm = x.shape[0]
    per = E // N_GROUPS
    scores = _sigmoid(logits.T[:E, :])
    biased = scores + bias_ref[...]
    neg = -jnp.inf
    b3 = biased.reshape(N_GROUPS, per, tm)
    i3 = lax.broadcasted_iota(jnp.int32, b3.shape, 1)
    m1 = jnp.max(b3, axis=1, keepdims=True)
    first = jnp.min(jnp.where(b3 == m1, i3, per), axis=1, keepdims=True)
    m2 = jnp.max(jnp.where(i3 == first, neg, b3), axis=1, keepdims=True)
    gscore = (m1 + m2).reshape(N_GROUPS, tm)
    gi = lax.broadcasted_iota(jnp.int32, gscore.shape, 0)
    gsel = jnp.zeros(gscore.shape, F32)
    work = gscore
    for _ in range(TOPK_GROUPS):
        m = jnp.max(work, axis=0, keepdims=True)
        pick = jnp.min(jnp.where(work == m, gi, N_GROUPS), axis=0, keepdims=True)
        hit = gi == pick
        gsel = jnp.where(hit, 1.0, gsel)
        work = jnp.where(hit, neg, work)
    keep = jnp.broadcast_to(gsel.reshape(N_GROUPS, 1, tm), b3.shape).reshape(E, tm)
    masked = jnp.where(keep > 0.0, biased, neg)
    ei = lax.broadcasted_iota(jnp.int32, masked.shape, 0)
    ids, gates = [], []
    for _ in range(TOP_K):
        m = jnp.max(masked, axis=0, keepdims=True)
        pick = jnp.min(jnp.where(masked == m, ei, E), axis=0, keepdims=True)
        hit = ei == pick
        ids.append(pick)
        gates.append(jnp.sum(jnp.where(hit, scores, 0.0), axis=0, keepdims=True))
        masked = jnp.where(hit, neg, masked)
    g = jnp.concatenate(gates, axis=0)
    idx_ref[...] = jnp.concatenate(ids, axis=0)
    gate_ref[...] = g / jnp.sum(g, axis=0, keepdims=True) * ROUTED_SCALE


def _router(hf, w_router, bias, tm=256):
    T, D = hf.shape
    E = w_router.shape[1]
    wpad = jnp.pad(w_router, ((0, 0), (0, LANE - E)))
    whi = wpad.astype(BF16)
    wlo = (wpad - whi.astype(F32)).astype(BF16)
    wspec = pl.BlockSpec((D, LANE), lambda i: (0, 0))
    ospec = pl.BlockSpec((TOP_K, tm), lambda i: (0, i))
    return pl.pallas_call(
        _router_kernel, grid=(T // tm,),
        in_specs=[pl.BlockSpec((tm, D), lambda i: (i, 0)), wspec, wspec, pl.BlockSpec((E, 1), lambda i: (0, 0))],
        out_specs=[ospec, ospec],
        out_shape=[jax.ShapeDtypeStruct((TOP_K, T), jnp.int32), jax.ShapeDtypeStruct((TOP_K, T), F32)],
        compiler_params=_params("arbitrary"), name="router",
    )(hf, whi, wlo, bias.reshape(E, 1))


def _moe_kernel(be_ref, pos0_ref, lo_ref, nv_ref, sd_ref, x_hbm, w1_ref, w3_ref, w2_ref, y_hbm, xbuf, ybuf,
                gsem, ssem, *, n_tok, n_chunks):
    i = pl.program_id(0)
    nb = pl.num_programs(0)
    R, D2 = xbuf.shape[1], xbuf.shape[2]
    F = w2_ref.shape[0]
    cw = D2 // n_chunks
    per = R // n_chunks
    tok_of = (lambda d: d & (n_tok - 1)) if n_tok & (n_tok - 1) == 0 else (lambda d: lax.rem(d, n_tok))

    nxt = jnp.minimum(i + 1, nb - 1)
    prv = jnp.maximum(i - 1, 0)
    g_pos0 = pos0_ref[nxt]
    s_pos0 = pos0_ref[prv]
    s_lo = lo_ref[prv]
    s_nv = jnp.where(i > 0, nv_ref[prv], 0)

    def gather_rows(pos0, sl, r0, r1):
        for r in range(r0, r1):
            tok = tok_of(sd_ref[pos0 + r])
            pltpu.make_async_copy(x_hbm.at[pl.ds(tok, 1), :], xbuf.at[sl, pl.ds(r, 1), :], gsem.at[sl]).start()

    def gather_wait(sl):
        for r in range(R):
            pltpu.make_async_copy(x_hbm.at[pl.ds(0, 1), :], xbuf.at[sl, pl.ds(r, 1), :], gsem.at[sl]).wait()

    def scatter_rows(pos0, lo, nv, sl, r0, r1):
        spare = TOP_K * n_tok + sl * R
        for r in range(r0, r1):
            valid = (r - lo).astype(U32) < nv.astype(U32)
            dest = jnp.where(valid, sd_ref[pos0 + r], spare + r)
            pltpu.make_async_copy(ybuf.at[sl, pl.ds(r, 1), :], y_hbm.at[pl.ds(dest, 1), :], ssem.at[sl]).start()

    def scatter_wait(sl):
        for r in range(R):
            pltpu.make_async_copy(ybuf.at[sl, pl.ds(r, 1), :], y_hbm.at[pl.ds(0, 1), :], ssem.at[sl]).wait()

    def step(slot):
        other = 1 - slot
        if slot == 0:
            @pl.when(i == 0)
            def _():
                ybuf[...] = jnp.zeros(ybuf.shape, U32)
                gather_rows(pos0_ref[0], 0, 0, R)

        gather_wait(slot)
        acc = None
        for c in range(n_chunks):
            gather_rows(g_pos0, other, c * per, (c + 1) * per)
            lo, hi = _unpack_pair(xbuf[slot, :, c * cw:(c + 1) * cw])
            xc = jnp.concatenate([lo.astype(BF16), hi.astype(BF16)], axis=1)
            ra = slice(c * cw, (c + 1) * cw)
            rb = slice(D2 + c * cw, D2 + (c + 1) * cw)
            wc = jnp.concatenate([jnp.concatenate([w1_ref[ra, :], w3_ref[ra, :]], axis=1),
                                  jnp.concatenate([w1_ref[rb, :], w3_ref[rb, :]], axis=1)], axis=0).astype(BF16)
            part = jnp.dot(xc, wc, preferred_element_type=F32)
            acc = part if acc is None else acc + part
        h1 = acc[:, :F]
        hid = (h1 * _sigmoid(h1) * acc[:, F:]).astype(BF16)

        @pl.when(i >= 1)
        def _():
            scatter_wait(slot)

        for c in range(n_chunks):
            scatter_rows(s_pos0, s_lo, s_nv, other, c * per, (c + 1) * per)
            wc = jnp.concatenate([w2_ref[:, c * cw:(c + 1) * cw], w2_ref[:, D2 + c * cw:D2 + (c + 1) * cw]],
                                 axis=1).astype(BF16)
            yc = jnp.dot(hid, wc, preferred_element_type=F32)
            ybuf[slot, :, c * cw:(c + 1) * cw] = _pack_pair(yc[:, :cw], yc[:, cw:])

        @pl.when(i == nb - 1)
        def _():
            scatter_rows(pos0_ref[i], lo_ref[i], nv_ref[i], slot, 0, R)
            gather_wait(other)
            scatter_wait(other)
            scatter_wait(slot)

    parity = lax.rem(i, 2)
    for s in (0, 1):
        pl.when(parity == s)(functools.partial(step, s))


def _moe(hp, plan, w_e1, w_e3, w_e2, layer, n_blocks):
    T, D2 = hp.shape
    _, E, D, F = w_e1.shape
    n_chunks = max(1, min(8, D2 // (2 * LANE)))
    wmap = lambda i, be, p0, lo, nv, sd: (layer, be[i], 0, 0)
    grid_spec = pltpu.PrefetchScalarGridSpec(
        num_scalar_prefetch=5, grid=(n_blocks,),
        in_specs=[pl.BlockSpec(memory_space=pl.ANY),
                  pl.BlockSpec((None, None, D, F), wmap),
                  pl.BlockSpec((None, None, D, F), wmap),
                  pl.BlockSpec((None, None, F, D), wmap)],
        out_specs=pl.BlockSpec(memory_space=pl.ANY),
        scratch_shapes=[pltpu.VMEM((2, MOE_ROWS, D2), U32), pltpu.VMEM((2, MOE_ROWS, D2), U32),
                        pltpu.SemaphoreType.DMA((2,)), pltpu.SemaphoreType.DMA((2,))])
    return pl.pallas_call(
        functools.partial(_moe_kernel, n_tok=T, n_chunks=n_chunks), grid_spec=grid_spec,
        out_shape=jax.ShapeDtypeStruct((TOP_K * T + 2 * MOE_ROWS, D2), U32),
        compiler_params=_params("arbitrary"), name="moe_experts",
    )(*plan, hp, w_e1, w_e3, w_e2)


def _route_plan(idx_t, n_experts, n_blocks):
    K, T = idx_t.shape
    e_flat = idx_t.T.reshape(T * K)
    sorted_e, sorted_m = lax.sort((e_flat, jnp.arange(T * K, dtype=jnp.int32)), num_keys=1, is_stable=True)
    edges = jnp.arange(n_experts + 1, dtype=jnp.int32)
    start = jnp.sum((sorted_e[None, :] < edges[:, None]).astype(jnp.int32), axis=1)
    counts = start[1:] - start[:-1]
    nblk = (counts + MOE_ROWS - 1) // MOE_ROWS
    bend = jnp.cumsum(nblk)
    j = jnp.arange(n_blocks, dtype=jnp.int32)
    used = j < bend[-1]
    be = jnp.minimum(jnp.sum((bend[None, :] <= j[:, None]).astype(jnp.int32), axis=1), n_experts - 1)
    off = (j - (bend - nblk)[be]) * MOE_ROWS
    nv = jnp.where(used, jnp.clip(counts[be] - off, 0, MOE_ROWS), 0).astype(jnp.int32)
    first = jnp.where(used, start[be] + off, 0).astype(jnp.int32)
    pos0 = jnp.minimum(first, T * K - MOE_ROWS)
    lo = first - pos0
    sorted_dest = (sorted_m % K) * T + sorted_m // K
    return be, pos0, lo, nv, sorted_dest


def _combine_kernel(*refs, alpha):
    hf_ref, hp_ref, gate_ref = refs[:3]
    y_refs = refs[3:3 + TOP_K]
    w13_ref, w2_ref, lg_ref, lb_ref, of_ref, ob_ref = refs[3 + TOP_K:]
    D2 = hp_ref.shape[1]
    F = w2_ref.shape[0]
    xlo, xhi = _unpack_pair(hp_ref[...])
    h13 = (jnp.dot(xlo.astype(BF16), w13_ref[:D2, :], preferred_element_type=F32)
           + jnp.dot(xhi.astype(BF16), w13_ref[D2:, :], preferred_element_type=F32))
    h1 = h13[:, :F]
    hid = (h1 * _sigmoid(h1) * h13[:, F:]).astype(BF16)
    sh = jnp.dot(hid, w2_ref[...], preferred_element_type=F32)
    acc_lo = sh[:, :D2]
    acc_hi = sh[:, D2:]
    gate = gate_ref[...]
    for k in range(TOP_K):
        ylo, yhi = _unpack_pair(y_refs[k][...])
        gk = gate[:, k:k + 1]
        acc_lo = acc_lo + ylo * gk
        acc_hi = acc_hi + yhi * gk
    of_ref[:, :D2] = alpha * hf_ref[:, :D2] + acc_lo
    of_ref[:, D2:] = alpha * hf_ref[:, D2:] + acc_hi
    y = _layer_norm(of_ref[...], lg_ref[...], lb_ref[...])
    of_ref[...] = y
    ob_ref[...] = y.astype(BF16)


def _combine(hf, hp, gate, yk, w13, w2, lg, lb, alpha, tm=128):
    T, D = hf.shape
    D2 = D // 2
    F2 = w13.shape[1]
    nt = T // tm
    row = pl.BlockSpec((tm, D), lambda i: (i, 0))
    vecd = pl.BlockSpec((1, D), lambda i: (0, 0))
    y_specs = [pl.BlockSpec((tm, D2), functools.partial(lambda i, k: (k * nt + i, 0), k=k)) for k in range(TOP_K)]
    return pl.pallas_call(
        functools.partial(_combine_kernel, alpha=alpha), grid=(nt,),
        in_specs=[row, pl.BlockSpec((tm, D2), lambda i: (i, 0)), pl.BlockSpec((tm, TOP_K), lambda i: (i, 0))]
        + y_specs + [pl.BlockSpec((D, F2), lambda i: (0, 0)), pl.BlockSpec((F2 // 2, D), lambda i: (0, 0)), vecd, vecd],
        out_specs=[row, row],
        out_shape=[jax.ShapeDtypeStruct((T, D), F32), jax.ShapeDtypeStruct((T, D), BF16)],
        compiler_params=_params("arbitrary"), name="moe_combine_ln",
    )(hf, hp, gate, *([yk] * TOP_K), w13, w2, lg.reshape(1, D), lb.reshape(1, D))


def _rot_half_cols(w):
    half = w.shape[-1] // 2
    return jnp.concatenate([-w[..., half:], w[..., :half]], axis=-1)


def _pad_cols(w, n):
    return jnp.pad(w, ((0, 0), (0, n - w.shape[1])))


def kernel(x, mem, positions, ln_in_g, ln_in_b, w_in, q_norm_g, kv_norm_g, w_q_b, w_kv_b, conv_w, conv_b, w_rg_a, b_rg_a, w_rg_x, b_rg_x, lru_lambda, mla_out_g, rnn_out_g, w_o, ln1_g, ln1_b, w_xq, w_xk, w_xv, w_xo, ln2_g, ln2_b, w_router, router_bias, w_e1, w_e3, w_e2, w_s1, w_s3, w_s2, ln3_g, ln3_b):
    B, S, D = x.shape
    M = mem.shape[1]
    L = w_in.shape[0]
    T = B * S
    q_lora = q_norm_g.shape[1]
    kv_lora = kv_norm_g.shape[1]
    H, qk_dim = w_q_b.shape[2], w_q_b.shape[3]
    C = conv_b.shape[1]
    rope = w_in.shape[2] - q_lora - kv_lora - 2 * C
    nope = qk_dim - rope
    E = w_router.shape[2]
    assert nope == LANE and rope == LANE // 2 and w_kv_b.shape[3] - nope == LANE
    assert T * TOP_K >= MOE_ROWS
    alpha = (2.0 * L) ** 0.25
    n_blocks = -(-(T * TOP_K) // MOE_ROWS) + E

    off_kv, off_kr = q_lora, q_lora + kv_lora
    off_u, off_g = off_kr + rope, off_kr + rope + C
    col_kv, col_q, col_kr = 2 * C, 2 * C + kv_lora, 2 * C + kv_lora + q_lora
    assert col_kv % kv_lora == 0 and col_q % q_lora == 0 and col_kr % LANE == 0
    n_proj = -(-(col_kr + LANE) // 512) * 512

    cs = _rope_tables(positions, rope)
    hf, hb = _ln_in(x.reshape(T, D), ln_in_g, ln_in_b)
    memf = mem.reshape(B * M, D)

    for l in range(L):
        wi = w_in[l]
        w_kr = wi[:, off_kr:off_kr + rope]
        w_proj = _pad_cols(jnp.concatenate(
            [wi[:, off_u:off_u + C], wi[:, off_g:off_g + C], wi[:, off_kv:off_kv + kv_lora], wi[:, :q_lora],
             w_kr, _rot_half_cols(w_kr)], axis=1), n_proj).astype(BF16)
        wq3 = w_q_b[l]
        wq = jnp.concatenate([wq3[..., :nope], wq3[..., nope:], _rot_half_cols(wq3[..., nope:])],
                             axis=-1).reshape(q_lora, H * 2 * LANE).astype(BF16)
        wkv3 = w_kv_b[l]
        wkv = jnp.concatenate([wkv3[..., :nope].reshape(kv_lora, H * LANE),
                               wkv3[..., nope:].reshape(kv_lora, H * LANE)], axis=1).astype(BF16)

        proj = _matmul(hb, w_proj, tm=min(1024, T), tn=512, name="in_proj")
        q = _q_up(proj, col_q // q_lora, q_norm_g[l], cs, wq, qk_dim ** -0.5, tm=min(512, T))
        kv, kr = _kv_up(proj, col_kv // kv_lora, col_kr // LANE, kv_norm_g[l], cs, wkv, tm=min(512, T))
        attn = _attention(q, kv, kr, B, S, H)
        rec = _rglru(proj, B, S, C, conv_w[l].reshape(conv_w.shape[1], C), conv_b[l],
                     w_rg_a[l].astype(BF16), b_rg_a[l], w_rg_x[l].astype(BF16), b_rg_x[l], lru_lambda[l])
        hf, hb = _out_proj(attn, rec, mla_out_g[l], rnn_out_g[l], hf, w_o[l].astype(BF16), ln1_g[l], ln1_b[l], alpha)

        w_mkv = jnp.concatenate([w_xk[l], w_xv[l]], axis=1).astype(BF16)
        kvm = _matmul(memf, w_mkv, tm=min(256, B * M), tn=w_mkv.shape[1], name="mem_kv")
        hf, hp = _xattn(hb, hf, kvm, w_xq[l].astype(BF16), w_xo[l].astype(BF16), ln2_g[l], ln2_b[l], alpha, S, M)

        idx_t, gate_t = _router(hf, w_router[l], router_bias[l])
        yk = _moe(hp, _route_plan(idx_t, E, n_blocks), w_e1, w_e3, w_e2, l, n_blocks)
        ws13 = jnp.concatenate([w_s1[l], w_s3[l]], axis=-1).astype(BF16)
        hf, hb = _combine(hf, hp, gate_t.T, yk, ws13, w_s2[l].astype(BF16), ln3_g[l], ln3_b[l], alpha)

    return hf.reshape(B, S, D)
```
